```python
import math
import jax, jax.numpy as jnp
from jax import lax
import numpy as np

D_MODEL = 2048
BATCH = 4
SEQ = 2048
DEPTH = 2
DEC_BATCH = 128
DEC_SEQ = 4
PAST_LEN = 2048
PAGE_SIZE = 128

H_A = 8
KV_A = 2
G_A = H_A // KV_A
DH_A = 128
W_A = H_A * DH_A
CMP_LEN = 32
CMP_STRIDE = 16
SEL_BLOCK = 64
SEL_TOP = 16
WINDOW = 512
FORCE_BONUS = 1.0e4
H_B = 8
DH_B = 128
W_B = H_B * DH_B
H_I = 16
D_I = DH_B
TOPK_MAX = 256
H_C = 8
D_C = 64
W_C = H_C * 2 * D_C
QBLK = 128
ALPHA = (2 * DEPTH) ** 0.25
BETA = (8 * DEPTH) ** -0.25
EPS = 1e-5
SPLIT_SIZES = (W_A, 6 * KV_A * DH_A, 3 * H_A, W_A, W_B, 2 * DH_B, H_I * D_I, D_I, H_I, W_B, W_C, W_C, W_C, W_C, 3 * D_MODEL)
N_IN = sum(SPLIT_SIZES)

kernel_name = 'nsa_dsa_diff_hybrid_step'

F32 = jnp.float32


def _alibi_slopes(n):
    return jnp.asarray([2.0 ** (-8.0 * (i + 1) / n) for i in range(n)], F32)


def _masked_softmax(s, mask):
    s = jnp.where(mask, s.astype(F32), -jnp.inf)
    m = jnp.max(s, axis=-1, keepdims=True)
    m = jnp.where(jnp.isfinite(m), m, 0.0)
    e = jnp.where(mask, jnp.exp(s - m), 0.0)
    return e / jnp.maximum(jnp.sum(e, axis=-1, keepdims=True), 1e-30)


def _layernorm(x, g, b):
    xf = x.astype(F32)
    mu = jnp.mean(xf, axis=-1, keepdims=True)
    xc = xf - mu
    y = xc * lax.rsqrt(jnp.mean(xc * xc, axis=-1, keepdims=True) + EPS)
    return (y * g + b).astype(x.dtype)


def _map_qblocks(fn, q_pos, *arrs):
    n = q_pos.shape[0]
    qb = QBLK if n % QBLK == 0 else n
    nb = n // qb
    blk = lambda a: jnp.swapaxes(a.reshape((a.shape[0], nb, qb) + a.shape[2:]), 0, 1)
    xs = (jnp.arange(nb, dtype=jnp.int32), q_pos.reshape(nb, qb)) + tuple(blk(a) for a in arrs)
    out = lax.map(lambda t: fn(*t), xs)
    unblk = lambda o: jnp.swapaxes(o, 0, 1).reshape((o.shape[1], n) + o.shape[3:])
    return jax.tree_util.tree_map(unblk, out)


def _cmp_to_sel(n_cmp, n_sel):
    cs = np.arange(n_cmp)[:, None] * CMP_STRIDE
    ss = np.arange(n_sel)[None, :] * SEL_BLOCK
    ov = np.clip(np.minimum(cs + CMP_LEN, ss + SEL_BLOCK) - np.maximum(cs, ss), 0, None)
    return jnp.asarray(ov / CMP_LEN, F32)


def _gather_pages(pool, page_table):
    g = pool[page_table]
    return g.reshape((g.shape[0], g.shape[1] * g.shape[2]) + g.shape[3:])


def _nsa(q, gates, kv_full, win_band, win_pos, q_pos, w_cmp, cmp_pos):
    B, Sq = q.shape[:2]
    L = kv_full.shape[1]
    n_sel = -(-L // SEL_BLOCK)
    Lp = n_sel * SEL_BLOCK
    kvp = jnp.pad(kv_full, ((0, 0), (0, Lp - L), (0, 0), (0, 0), (0, 0)))
    slopes = _alibi_slopes(H_A).reshape(KV_A, G_A)
    scale = DH_A ** -0.5
    qg = q.reshape(B, Sq, KV_A, G_A, DH_A)
    R = CMP_LEN // CMP_STRIDE
    n_ch = Lp // CMP_STRIDE
    n_cmp = n_ch - R + 1
    chunks = kvp[:, :, :2].reshape(B, n_ch, CMP_STRIDE, 2, KV_A, DH_A)
    pos = jnp.swapaxes(cmp_pos, 0, 1)[:, :, None, :]
    kc = None
    for r in range(R):
        sl = slice(r * CMP_STRIDE, (r + 1) * CMP_STRIDE)
        term = jnp.einsum('bjlckd,clde->bjcke', chunks[:, r:r + n_cmp] + pos[sl], w_cmp[:, sl])
        kc = term if kc is None else kc + term
    cend = jnp.arange(n_cmp, dtype=jnp.int32) * CMP_STRIDE + (CMP_LEN - 1)
    dist_c = q_pos[:, None] - cend[None, :]
    s_c = jnp.einsum('bqcgd,bjcd->bcgqj', qg, kc[:, :, 0]).astype(F32) * scale - slopes[:, :, None, None] * dist_c.astype(F32)
    p_c = _masked_softmax(s_c, dist_c >= 0)
    o_cmp = jnp.einsum('bcgqj,bjcd->bqcgd', p_c, kc[:, :, 1])
    imp = jnp.einsum('bcgqj,ji->bcqi', p_c, _cmp_to_sel(n_cmp, n_sel))
    ib = jnp.arange(n_sel, dtype=jnp.int32)[None, :]
    cur = (q_pos // SEL_BLOCK)[:, None]
    valid = ib * SEL_BLOCK <= q_pos[:, None]
    forced = (ib == 0) | (ib == cur) | (ib == cur - 1)
    imp = jnp.where(valid, imp + jnp.where(forced, FORCE_BONUS, 0.0), -jnp.inf)
    _, sel = lax.top_k(imp, min(SEL_TOP, n_sel))
    sel = jnp.swapaxes(sel, 1, 2)
    kvs = kvp[:, :, 2:].reshape(B, n_sel, SEL_BLOCK, 2, KV_A, DH_A).transpose(0, 4, 1, 2, 3, 5)
    offs = jnp.arange(SEL_BLOCK, dtype=jnp.int32)
    gather = jax.vmap(jax.vmap(lambda a, i: a[i], in_axes=(0, 1), out_axes=1))

    def blk(bi, qp, q_b, sel_b):
        qb = qp.shape[0]
        g = gather(kvs, sel_b)
        kpos = sel_b[..., None] * SEL_BLOCK + offs
        dist = jnp.moveaxis(qp[None, :, None, None, None] - kpos, 2, 1)[:, :, None]
        s = jnp.einsum('bqcgd,bqckpd->bcgqkp', q_b, g[..., 0, :]).astype(F32) * scale - slopes[None, :, :, None, None, None] * dist.astype(F32)
        shp = s.shape
        flat = shp[:-2] + (shp[-2] * shp[-1],)
        p = _masked_softmax(s.reshape(flat), jnp.broadcast_to(dist >= 0, shp).reshape(flat)).reshape(shp)
        o_slc = jnp.einsum('bcgqkp,bqckpd->bqcgd', p, g[..., 1, :])
        kw = lax.dynamic_slice_in_dim(win_band, bi * qb, qb + WINDOW, axis=1)
        pw = lax.dynamic_slice_in_dim(win_pos, bi * qb, qb + WINDOW)
        dw = qp[:, None] - pw[None, :]
        mw = (pw[None, :] >= 0) & (dw >= 0) & (dw <= WINDOW)
        sw = jnp.einsum('bqcgd,bkcd->bcgqk', q_b, kw[:, :, 0]).astype(F32) * scale - slopes[:, :, None, None] * dw.astype(F32)
        o_win = jnp.einsum('bcgqk,bkcd->bqcgd', _masked_softmax(sw, mw), kw[:, :, 1])
        return o_slc, o_win

    o_slc, o_win = _map_qblocks(blk, q_pos, qg, sel)
    g = jax.nn.sigmoid(gates.astype(F32)).reshape(B, Sq, KV_A, G_A, 3)
    o = g[..., 0, None] * o_cmp + g[..., 1, None] * o_slc + g[..., 2, None] * o_win
    return o.reshape(B, Sq, W_A).astype(q.dtype)


def _dsa(q, iq, iw, kvi, q_pos):
    L = kvi.shape[1]
    k_sel = min(TOPK_MAX, L // 4)
    slopes = _alibi_slopes(H_B)
    kidx = kvi[:, :, 2]
    kv = kvi[:, :, :2]
    kpos = jnp.arange(L, dtype=jnp.int32)
    gather = jax.vmap(lambda a, i: a[i])

    def blk(bi, qp, q_b, iq_b, iw_b):
        rel = jax.nn.relu(jnp.einsum('bqhd,bld->bqhl', iq_b, kidx).astype(F32) * D_I ** -0.5)
        score = jnp.einsum('bqh,bqhl->bql', iw_b.astype(F32) * H_I ** -0.5, rel)
        score = jnp.where(kpos[None, None, :] <= qp[None, :, None], score, -jnp.inf)
        _, idx = lax.top_k(score, k_sel)
        g = gather(kv, idx)
        dist = qp[None, :, None] - idx
        s = jnp.einsum('bqhd,bqkd->bhqk', q_b, g[..., 0, :]).astype(F32) * DH_B ** -0.5 - slopes[None, :, None, None] * dist[:, None].astype(F32)
        p = _masked_softmax(s, (dist >= 0)[:, None])
        return jnp.einsum('bhqk,bqkd->bqhd', p, g[..., 1, :])

    return _map_qblocks(blk, q_pos, q, iq, iw).astype(q.dtype)


def _diff(q, kv, q_pos, lam, sub_g, li):
    L = kv.shape[1]
    kpos = jnp.arange(L, dtype=jnp.int32)
    lam_init = 0.8 - 0.6 * math.exp(-0.3 * li)
    lf = lam.astype(F32)
    lam_val = jnp.exp(jnp.sum(lf[0] * lf[1])) - jnp.exp(jnp.sum(lf[2] * lf[3])) + lam_init
    slopes = _alibi_slopes(H_C)
    k = kv[:, :, 0]
    v = kv[:, :, 1]
    scale = D_C ** -0.5

    def blk(bi, qp, q_b):
        dist = qp[:, None] - kpos[None, :]
        bias = -slopes[:, None, None] * dist.astype(F32)
        mask = dist >= 0
        s1 = jnp.einsum('bqhd,bkhd->bhqk', q_b[..., :D_C], k[..., :D_C]).astype(F32) * scale + bias
        s2 = jnp.einsum('bqhd,bkhd->bhqk', q_b[..., D_C:], k[..., D_C:]).astype(F32) * scale + bias
        a = _masked_softmax(s1, mask) - lam_val * _masked_softmax(s2, mask)
        return jnp.einsum('bhqk,bkhd->bqhd', a, v)

    o = _map_qblocks(blk, q_pos, q).astype(F32)
    o = o * lax.rsqrt(jnp.mean(o * o, axis=-1, keepdims=True) + EPS) * sub_g * (1.0 - lam_init)
    return o.astype(q.dtype)


def _project(x, w_in):
    h = jnp.einsum('bsd,de->bse', x, w_in)
    idx = [int(v) for v in np.cumsum(SPLIT_SIZES)[:-1]]
    return jnp.split(h, idx, axis=-1)


def _new_rows(parts, B, S):
    a_kv = parts[1].reshape(B, S, 6, KV_A, DH_A)
    nsa_rows = a_kv[:, :, :4]
    win_rows = a_kv[:, :, 4:]
    dsa_rows = jnp.concatenate([parts[5].reshape(B, S, 2, DH_B), parts[7][:, :, None, :]], axis=2)
    diff_rows = jnp.stack([parts[11].reshape(B, S, H_C, 2 * D_C), parts[12].reshape(B, S, H_C, 2 * D_C)], axis=2)
    return nsa_rows, win_rows, dsa_rows, diff_rows


def _layer(x, parts, nsa_full, win_band, win_pos, dsa_full, diff_full, q_pos, li,
           w_cmp, cmp_pos, lam, sub_g, w_branch, w_out, ln_g, ln_b):
    B, S, _ = x.shape
    a_q, a_g, a_z = parts[0], parts[2], parts[3]
    b_q, b_iq, b_iw, b_z = parts[4], parts[6], parts[8], parts[9]
    c_q, c_z, mg = parts[10], parts[13], parts[14]
    o_a = _nsa(a_q.reshape(B, S, H_A, DH_A), a_g.reshape(B, S, H_A, 3), nsa_full, win_band, win_pos, q_pos, w_cmp, cmp_pos)
    o_b = _dsa(b_q.reshape(B, S, H_B, DH_B), b_iq.reshape(B, S, H_I, D_I), b_iw, dsa_full, q_pos)
    o_c = _diff(c_q.reshape(B, S, H_C, 2 * D_C), diff_full, q_pos, lam, sub_g, li)
    u = jnp.stack([o_a * jax.nn.silu(a_z),
                   o_b.reshape(B, S, W_B) * jax.nn.silu(b_z),
                   o_c.reshape(B, S, W_C) * jax.nn.silu(c_z)], axis=2)
    br = jnp.einsum('bsnw,nwd->bsnd', u, w_branch)
    m = jnp.sum(jax.nn.sigmoid(mg.reshape(B, S, 3, D_MODEL)) * br, axis=2)
    y = jnp.einsum('bsd,de->bse', m, w_out)
    return _layernorm(ALPHA * x + y, ln_g, ln_b)


def setup_inputs(seed: int = 0) -> dict:
    key = jax.random.key(seed)
    ks = jax.random.split(key, 18)
    n_pages = PAST_LEN // PAGE_SIZE
    n_used = DEC_BATCH * n_pages
    n_phys = n_used + (n_used + 3) // 4
    nrm = lambda k, shp, sc=1.0: jax.random.normal(k, shp, F32) * sc
    page_table = jax.random.permutation(ks[6], n_phys)[:n_used].reshape(DEC_BATCH, n_pages).astype(jnp.int32)
    return {
        'x_prompt': nrm(ks[0], (BATCH, SEQ, D_MODEL)),
        'x_sample': nrm(ks[1], (DEC_BATCH, DEC_SEQ, D_MODEL)),
        'cache_nsa': nrm(ks[2], (DEPTH, n_phys, PAGE_SIZE, 4, KV_A, DH_A)),
        'cache_dsa': nrm(ks[3], (DEPTH, n_phys, PAGE_SIZE, 3, DH_B)),
        'cache_diff': nrm(ks[4], (DEPTH, n_phys, PAGE_SIZE, 2, H_C, 2 * D_C)),
        'state_nsa_win': nrm(ks[5], (DEPTH, DEC_BATCH, min(WINDOW, PAST_LEN), 2, KV_A, DH_A)),
        'page_table': page_table,
        'w_in': nrm(ks[7], (DEPTH, D_MODEL, N_IN), D_MODEL ** -0.5),
        'w_cmp': nrm(ks[8], (DEPTH, 2, CMP_LEN, DH_A, DH_A), (CMP_LEN * DH_A) ** -0.5),
        'cmp_pos': nrm(ks[9], (DEPTH, 2, CMP_LEN, DH_A), 0.1),
        'lam': nrm(ks[10], (DEPTH, 4, D_C), 0.1),
        'sub_g': 1.0 + nrm(ks[11], (DEPTH, 2 * D_C), 0.02),
        'w_branch': nrm(ks[12], (DEPTH, 3, W_A, D_MODEL), W_A ** -0.5 * BETA),
        'w_out': nrm(ks[13], (DEPTH, D_MODEL, D_MODEL), D_MODEL ** -0.5 * BETA),
        'ln_g': 1.0 + nrm(ks[14], (DEPTH, D_MODEL), 0.02),
        'ln_b': nrm(ks[15], (DEPTH, D_MODEL), 0.02),
    }


def reference(x_prompt, x_sample, cache_nsa, cache_dsa, cache_diff, state_nsa_win, page_table,
              w_in, w_cmp, cmp_pos, lam, sub_g, w_branch, w_out, ln_g, ln_b):
    B, S, _ = x_prompt.shape
    DB, DS, _ = x_sample.shape
    past = page_table.shape[1] * cache_nsa.shape[2]
    wbuf = state_nsa_win.shape[2]
    q_pos_p = jnp.arange(S, dtype=jnp.int32)
    win_pos_p = jnp.concatenate([jnp.full((WINDOW,), -1, jnp.int32), q_pos_p])
    q_pos_s = past + jnp.arange(DS, dtype=jnp.int32)
    win_pos_s = jnp.concatenate([jnp.full((WINDOW - wbuf,), -1, jnp.int32),
                                 past - wbuf + jnp.arange(wbuf, dtype=jnp.int32), q_pos_s])
    xp, xs = x_prompt, x_sample
    nsa_p, nsa_s, dsa_p, dsa_s, diff_p, diff_s, win_p, win_s = [], [], [], [], [], [], [], []
    for li in range(DEPTH):
        lw = (w_cmp[li], cmp_pos[li], lam[li], sub_g[li], w_branch[li], w_out[li], ln_g[li], ln_b[li])
        pp = _project(xp, w_in[li])
        nsa_r, win_r, dsa_r, diff_r = _new_rows(pp, B, S)
        band_p = jnp.concatenate([jnp.zeros((B, WINDOW) + win_r.shape[2:], win_r.dtype), win_r], axis=1)
        xp = _layer(xp, pp, nsa_r, band_p, win_pos_p, dsa_r, diff_r, q_pos_p, li, *lw)
        nsa_p.append(nsa_r)
        dsa_p.append(dsa_r)
        diff_p.append(diff_r)
        win_p.append(win_r[:, S - min(WINDOW, S):])
        ps = _project(xs, w_in[li])
        nsa_n, win_n, dsa_n, diff_n = _new_rows(ps, DB, DS)
        nsa_full = jnp.concatenate([_gather_pages(cache_nsa[li], page_table), nsa_n], axis=1)
        dsa_full = jnp.concatenate([_gather_pages(cache_dsa[li], page_table), dsa_n], axis=1)
        diff_full = jnp.concatenate([_gather_pages(cache_diff[li], page_table), diff_n], axis=1)
        win_hist = jnp.concatenate([state_nsa_win[li], win_n], axis=1)
        band_s = jnp.concatenate([jnp.zeros((DB, WINDOW - wbuf) + win_n.shape[2:], win_n.dtype), win_hist], axis=1)
        xs = _layer(xs, ps, nsa_full, band_s, win_pos_s, dsa_full, diff_full, q_pos_s, li, *lw)
        nsa_s.append(nsa_n)
        dsa_s.append(dsa_n)
        diff_s.append(diff_n)
        win_s.append(win_hist[:, win_hist.shape[1] - wbuf:])
    return (xp, xs, jnp.stack(nsa_p), jnp.stack(nsa_s), jnp.stack(dsa_p), jnp.stack(dsa_s),
            jnp.stack(diff_p), jnp.stack(diff_s), jnp.stack(win_p), jnp.stack(win_s))
```

```python
import functools
import math

import jax
import jax.numpy as jnp
import numpy as np
from jax import lax
from jax.experimental import pallas as pl
from jax.experimental.pallas import tpu as pltpu

F32 = jnp.float32
BF16 = jnp.bfloat16

D_MODEL = 2048
DEPTH = 2
PAGE_SIZE = 128
H_A, KV_A, DH_A = 8, 2, 128
G_A = H_A // KV_A
W_A = H_A * DH_A
CMP_LEN, CMP_STRIDE = 32, 16
SEL_BLOCK, SEL_TOP = 64, 16
WINDOW = 512
FORCE_BONUS = 1.0e4
H_B, DH_B = 8, 128
W_B = H_B * DH_B
H_I, D_I = 16, 128
TOPK_MAX = 256
H_C, D_C = 8, 64
W_C = H_C * 2 * D_C
ALPHA = (2 * DEPTH) ** 0.25
EPS = 1e-5
SPLIT_SIZES = (W_A, 6 * KV_A * DH_A, 3 * H_A, W_A, W_B, 2 * DH_B, H_I * D_I, D_I, H_I, W_B, W_C, W_C, W_C, W_C, 3 * D_MODEL)
SPLIT_OFFS = tuple(int(v) for v in np.cumsum((0,) + SPLIT_SIZES[:-1]))

(P_AQ, P_AKV, P_AG, P_AZ, P_BQ, P_BKV, P_BIQ, P_BIK, P_BIW, P_BZ, P_CQ, P_CK, P_CV, P_CZ, P_MG) = range(15)
LAYOUT_ORDER = (P_AQ, P_AZ, P_BQ, P_BZ, P_CQ, P_CZ, P_MG, P_BIQ, P_CK, P_CV, P_AKV, P_BKV, P_BIK, P_AG, P_BIW)
LANE = 128
N_PROJ = 18432


def _layout_offsets():
    offs, o = {}, 0
    for p in LAYOUT_ORDER:
        offs[p] = o
        o += SPLIT_SIZES[p]
    assert o <= N_PROJ and N_PROJ % LANE == 0
    return offs


OFF = _layout_offsets()
VMEM_LIMIT = 48 * 1024 * 1024


def _alibi_slopes(n):
    return jnp.asarray([2.0 ** (-8.0 * (i + 1) / n) for i in range(n)], F32)


def _proj_kernel(x_ref, w_ref, o_ref, xb_ref):
    @pl.when(pl.program_id(1) == 0)
    def _():
        xb_ref[...] = x_ref[...].astype(BF16)

    o_ref[...] = jnp.dot(xb_ref[...], w_ref[...], preferred_element_type=F32)


def _project(x2d, w_bf16):
    m, k = x2d.shape
    n = w_bf16.shape[1]
    tm = next(t for t in (512, 256, 128, m) if m % t == 0)
    tn = 1024
    assert n % tn == 0
    return pl.pallas_call(
        _proj_kernel,
        grid=(m // tm, n // tn),
        in_specs=[pl.BlockSpec((tm, k), lambda i, j: (i, 0)),
                  pl.BlockSpec((k, tn), lambda i, j: (0, j))],
        out_specs=pl.BlockSpec((tm, tn), lambda i, j: (i, j)),
        out_shape=jax.ShapeDtypeStruct((m, n), F32),
        scratch_shapes=[pltpu.VMEM((tm, k), BF16)],
        compiler_params=pltpu.CompilerParams(dimension_semantics=("parallel", "arbitrary"),
                                             vmem_limit_bytes=VMEM_LIMIT),
        name="proj",
    )(x2d, w_bf16)


def _permute_w_in(w):
    cols = [w[:, SPLIT_OFFS[p]:SPLIT_OFFS[p] + SPLIT_SIZES[p]] for p in LAYOUT_ORDER]
    used = sum(SPLIT_SIZES)
    cols.append(jnp.zeros((w.shape[0], N_PROJ - used), w.dtype))
    return jnp.concatenate(cols, axis=1).astype(BF16)


def _piece(h, p):
    return h[..., OFF[p]:OFF[p] + SPLIT_SIZES[p]]


def _sigmoid(v):
    return 1.0 / (1.0 + jnp.exp(-v))


def _merge_kernel(oa_ref, ob_ref, oc_ref, za_ref, zb_ref, zc_ref, mg_ref, x_ref, wbr_ref, wout_ref, g_ref, b_ref, y_ref):
    m = None
    for n, (o_ref, z_ref) in enumerate(((oa_ref, za_ref), (ob_ref, zb_ref), (oc_ref, zc_ref))):
        z = z_ref[...]
        u = (o_ref[...] * (z * _sigmoid(z))).astype(BF16)
        br = jnp.dot(u, wbr_ref[n], preferred_element_type=F32)
        t = _sigmoid(mg_ref[:, n * D_MODEL:(n + 1) * D_MODEL]) * br
        m = t if m is None else m + t
    y = jnp.dot(m.astype(BF16), wout_ref[...], preferred_element_type=F32)
    r = ALPHA * x_ref[...] + y
    mu = jnp.mean(r, axis=-1, keepdims=True)
    rc = r - mu
    var = jnp.mean(rc * rc, axis=-1, keepdims=True)
    y_ref[...] = rc * lax.rsqrt(var + EPS) * g_ref[...] + b_ref[...]


def _merge(oa, ob, oc, h, x2d, wbr_bf16, wout_bf16, ln_g, ln_b):
    m = x2d.shape[0]
    tm = min(128, m)
    assert m % tm == 0
    w1 = W_A
    zspec = lambda p: pl.BlockSpec((tm, w1), lambda i, c=OFF[p] // w1: (i, c))
    ospec = pl.BlockSpec((tm, w1), lambda i: (i, 0))
    const = lambda shape: pl.BlockSpec(shape, lambda i: (0,) * len(shape), pipeline_mode=pl.Buffered(1))
    return pl.pallas_call(
        _merge_kernel,
        grid=(m // tm,),
        in_specs=[ospec, ospec, ospec, zspec(P_AZ), zspec(P_BZ), zspec(P_CZ),
                  pl.BlockSpec((tm, 3 * D_MODEL), lambda i, c=OFF[P_MG] // (3 * D_MODEL): (i, c)),
                  pl.BlockSpec((tm, D_MODEL), lambda i: (i, 0)),
                  const((3, w1, D_MODEL)), const((D_MODEL, D_MODEL)),
                  const((1, D_MODEL)), const((1, D_MODEL))],
        out_specs=pl.BlockSpec((tm, D_MODEL), lambda i: (i, 0)),
        out_shape=jax.ShapeDtypeStruct((m, D_MODEL), F32),
        compiler_params=pltpu.CompilerParams(dimension_semantics=("parallel",), vmem_limit_bytes=VMEM_LIMIT),
        name="merge",
    )(oa, ob, oc, h, h, h, h, x2d, wbr_bf16, wout_bf16, ln_g.reshape(1, -1), ln_b.reshape(1, -1))


def _masked_softmax(s, mask):
    s = jnp.where(mask, s.astype(F32), -jnp.inf)
    m = jnp.max(s, axis=-1, keepdims=True)
    m = jnp.where(jnp.isfinite(m), m, 0.0)
    e = jnp.where(mask, jnp.exp(s - m), 0.0)
    return e / jnp.maximum(jnp.sum(e, axis=-1, keepdims=True), 1e-30)


def _cmp_to_sel(n_cmp, n_sel):
    cs = np.arange(n_cmp)[:, None] * CMP_STRIDE
    ss = np.arange(n_sel)[None, :] * SEL_BLOCK
    ov = np.clip(np.minimum(cs + CMP_LEN, ss + SEL_BLOCK) - np.maximum(cs, ss), 0, None)
    return jnp.asarray(ov / CMP_LEN, F32)


def _topk_mask_small(v, k):
    n = v.shape[-1]
    vi = v[..., :, None]
    vj = v[..., None, :]
    before = jnp.arange(n)[None, :] < jnp.arange(n)[:, None]
    rank = jnp.sum((vj > vi) | ((vj == vi) & before), axis=-1)
    return rank < k


def _topk_mask_large(score, k):
    thr = lax.top_k(score, k)[0][..., -1:]
    gt = score > thr
    eq = score == thr
    need = k - jnp.sum(gt, axis=-1, keepdims=True)
    pref = jnp.cumsum(eq.astype(jnp.int32), axis=-1)
    return gt | (eq & (pref <= need))


def _nsa(q, gates, kv_full, win_kv, win_pos, q_pos, w_cmp, cmp_pos):
    B, Sq = q.shape[:2]
    L = kv_full.shape[1]
    n_sel = -(-L // SEL_BLOCK)
    Lp = n_sel * SEL_BLOCK
    kvp = jnp.pad(kv_full, ((0, 0), (0, Lp - L), (0, 0), (0, 0), (0, 0)))
    slopes = _alibi_slopes(H_A).reshape(KV_A, G_A)
    scale = DH_A ** -0.5
    qg = q.reshape(B, Sq, KV_A, G_A, DH_A)
    R = CMP_LEN // CMP_STRIDE
    n_ch = Lp // CMP_STRIDE
    n_cmp = n_ch - R + 1
    chunks = kvp[:, :, :2].reshape(B, n_ch, CMP_STRIDE, 2, KV_A, DH_A)
    pos = jnp.swapaxes(cmp_pos, 0, 1)[:, :, None, :]
    kc = None
    for r in range(R):
        sl = slice(r * CMP_STRIDE, (r + 1) * CMP_STRIDE)
        term = jnp.einsum('bjlckd,clde->bjcke', chunks[:, r:r + n_cmp] + pos[sl], w_cmp[:, sl])
        kc = term if kc is None else kc + term
    cend = jnp.arange(n_cmp, dtype=jnp.int32) * CMP_STRIDE + (CMP_LEN - 1)
    dist_c = q_pos[:, None] - cend[None, :]
    s_c = jnp.einsum('bqcgd,bjcd->bcgqj', qg, kc[:, :, 0]).astype(F32) * scale - slopes[:, :, None, None] * dist_c.astype(F32)
    p_c = _masked_softmax(s_c, dist_c >= 0)
    o_cmp = jnp.einsum('bcgqj,bjcd->bqcgd', p_c, kc[:, :, 1])
    imp = jnp.einsum('bcgqj,ji->bcqi', p_c, _cmp_to_sel(n_cmp, n_sel))
    ib = jnp.arange(n_sel, dtype=jnp.int32)[None, :]
    cur = (q_pos // SEL_BLOCK)[:, None]
    valid = ib * SEL_BLOCK <= q_pos[:, None]
    forced = (ib == 0) | (ib == cur) | (ib == cur - 1)
    imp = jnp.where(valid, imp + jnp.where(forced, FORCE_BONUS, 0.0), -jnp.inf)
    selm = _topk_mask_small(imp, min(SEL_TOP, n_sel))
    kpos = jnp.arange(Lp, dtype=jnp.int32)
    dist = q_pos[:, None] - kpos[None, :]
    keym = jnp.repeat(selm, SEL_BLOCK, axis=-1) & (dist >= 0)[None, None]
    s = jnp.einsum('bqcgd,blcd->bcgql', qg, kvp[:, :, 2]).astype(F32) * scale - slopes[None, :, :, None, None] * dist.astype(F32)
    p = _masked_softmax(s, jnp.broadcast_to(keym[:, :, None], s.shape))
    o_slc = jnp.einsum('bcgql,blcd->bqcgd', p, kvp[:, :, 3])
    dw = q_pos[:, None] - win_pos[None, :]
    mw = (win_pos[None, :] >= 0) & (dw >= 0) & (dw <= WINDOW)
    sw = jnp.einsum('bqcgd,bkcd->bcgqk', qg, win_kv[:, :, 0]).astype(F32) * scale - slopes[:, :, None, None] * dw.astype(F32)
    o_win = jnp.einsum('bcgqk,bkcd->bqcgd', _masked_softmax(sw, mw), win_kv[:, :, 1])
    g = _sigmoid(gates.astype(F32)).reshape(B, Sq, KV_A, G_A, 3)
    o = g[..., 0, None] * o_cmp + g[..., 1, None] * o_slc + g[..., 2, None] * o_win
    return o.reshape(B, Sq, W_A)


def _dsa(q, iq, iw, kvi, q_pos):
    L = kvi.shape[1]
    k_sel = min(TOPK_MAX, L // 4)
    slopes = _alibi_slopes(H_B)
    kidx = kvi[:, :, 2]
    kpos = jnp.arange(L, dtype=jnp.int32)
    rel = jax.nn.relu(jnp.einsum('bqhd,bld->bqhl', iq, kidx).astype(F32) * D_I ** -0.5)
    score = jnp.sum((iw.astype(F32) * H_I ** -0.5)[..., None] * rel, axis=2)
    dist = q_pos[:, None] - kpos[None, :]
    score = jnp.where((dist >= 0)[None], score, -jnp.inf)
    selm = _topk_mask_large(score, k_sel) & (dist >= 0)[None]
    s = jnp.einsum('bqhd,bld->bhql', q, kvi[:, :, 0]).astype(F32) * DH_B ** -0.5 - slopes[None, :, None, None] * dist[None, None].astype(F32)
    p = _masked_softmax(s, jnp.broadcast_to(selm[:, None], s.shape))
    return jnp.einsum('bhql,bld->bqhd', p, kvi[:, :, 1])


def _diff(q, kv, q_pos, lam, sub_g, li):
    L = kv.shape[1]
    kpos = jnp.arange(L, dtype=jnp.int32)
    lam_init = 0.8 - 0.6 * math.exp(-0.3 * li)
    lf = lam.astype(F32)
    lam_val = jnp.exp(jnp.sum(lf[0] * lf[1])) - jnp.exp(jnp.sum(lf[2] * lf[3])) + lam_init
    slopes = _alibi_slopes(H_C)
    k = kv[:, :, 0]
    v = kv[:, :, 1]
    scale = D_C ** -0.5
    dist = q_pos[:, None] - kpos[None, :]
    bias = -slopes[:, None, None] * dist.astype(F32)
    mask = jnp.broadcast_to(dist >= 0, (q.shape[0], H_C) + dist.shape)
    s1 = jnp.einsum('bqhd,bkhd->bhqk', q[..., :D_C], k[..., :D_C]).astype(F32) * scale + bias
    s2 = jnp.einsum('bqhd,bkhd->bhqk', q[..., D_C:], k[..., D_C:]).astype(F32) * scale + bias
    a = _masked_softmax(s1, mask) - lam_val * _masked_softmax(s2, mask)
    o = jnp.einsum('bhqk,bkhd->bqhd', a, v).astype(F32)
    o = o * lax.rsqrt(jnp.mean(o * o, axis=-1, keepdims=True) + EPS) * sub_g * (1.0 - lam_init)
    return o


def _gather_pages(pool, page_table):
    g = pool[page_table]
    return g.reshape((g.shape[0], g.shape[1] * g.shape[2]) + g.shape[3:])


def _new_rows(h, B, S):
    a_kv = _piece(h, P_AKV).reshape(B, S, 6, KV_A, DH_A)
    dsa_rows = h[..., OFF[P_BKV]:OFF[P_BIK] + D_I].reshape(B, S, 3, DH_B)
    diff_rows = h[..., OFF[P_CK]:OFF[P_CV] + W_C].reshape(B, S, 2, H_C, 2 * D_C)
    return a_kv[:, :, :4], a_kv[:, :, 4:], dsa_rows, diff_rows


def _mix(x, h, nsa_full, win_kv, win_pos, dsa_full, diff_full, q_pos, li, lw, wbr, wout):
    w_cmp, cmp_pos, lam, sub_g, ln_g, ln_b = lw
    B, S, _ = x.shape
    o_a = _nsa(_piece(h, P_AQ).reshape(B, S, H_A, DH_A), _piece(h, P_AG).reshape(B, S, H_A, 3),
               nsa_full, win_kv, win_pos, q_pos, w_cmp, cmp_pos)
    o_b = _dsa(_piece(h, P_BQ).reshape(B, S, H_B, DH_B), _piece(h, P_BIQ).reshape(B, S, H_I, D_I),
               _piece(h, P_BIW), dsa_full, q_pos)
    o_c = _diff(_piece(h, P_CQ).reshape(B, S, H_C, 2 * D_C), diff_full, q_pos, lam, sub_g, li)
    m = B * S
    y = _merge(o_a.reshape(m, W_A), o_b.reshape(m, W_B), o_c.reshape(m, W_C), h.reshape(m, N_PROJ),
               x.reshape(m, D_MODEL), wbr, wout, ln_g, ln_b)
    return y.reshape(B, S, D_MODEL)


def kernel(x_prompt, x_sample, cache_nsa, cache_dsa, cache_diff, state_nsa_win, page_table,
           w_in, w_cmp, cmp_pos, lam, sub_g, w_branch, w_out, ln_g, ln_b):
    B, S, _ = x_prompt.shape
    DB, DS, _ = x_sample.shape
    past = page_table.shape[1] * cache_nsa.shape[2]
    wbuf = state_nsa_win.shape[2]
    q_pos_p = jnp.arange(S, dtype=jnp.int32)
    q_pos_s = past + jnp.arange(DS, dtype=jnp.int32)
    win_pos_s = jnp.concatenate([past - wbuf + jnp.arange(wbuf, dtype=jnp.int32), q_pos_s])
    xp, xs = x_prompt, x_sample
    outs = [[] for _ in range(8)]
    for li in range(DEPTH):
        lw = (w_cmp[li], cmp_pos[li], lam[li], sub_g[li], ln_g[li], ln_b[li])
        w_l = _permute_w_in(w_in[li])
        wbr = w_branch[li].astype(BF16)
        wout = w_out[li].astype(BF16)
        hp = _project(xp.reshape(B * S, D_MODEL), w_l).reshape(B, S, N_PROJ)
        nsa_r, win_r, dsa_r, diff_r = _new_rows(hp, B, S)
        xp = _mix(xp, hp, nsa_r, win_r, q_pos_p, dsa_r, diff_r, q_pos_p, li, lw, wbr, wout)
        hs = _project(xs.reshape(DB * DS, D_MODEL), w_l).reshape(DB, DS, N_PROJ)
        nsa_n, win_n, dsa_n, diff_n = _new_rows(hs, DB, DS)
        nsa_full = jnp.concatenate([_gather_pages(cache_nsa[li], page_table), nsa_n], axis=1)
        dsa_full = jnp.concatenate([_gather_pages(cache_dsa[li], page_table), dsa_n], axis=1)
        diff_full = jnp.concatenate([_gather_pages(cache_diff[li], page_table), diff_n], axis=1)
        win_hist = jnp.concatenate([state_nsa_win[li], win_n], axis=1)
        xs = _mix(xs, hs, nsa_full, win_hist, win_pos_s, dsa_full, diff_full, q_pos_s, li, lw, wbr, wout)
        for lst, v in zip(outs, (nsa_r, nsa_n, dsa_r, dsa_n, diff_r, diff_n,
                                 win_r[:, S - min(WINDOW, S):], win_hist[:, win_hist.shape[1] - wbuf:])):
            lst.append(v)
    return (xp, xs) + tuple(jnp.stack(l) for l in outs)
```

```python
import functools
import math

import jax
import jax.numpy as jnp
import numpy as np
from jax import lax
from jax.experimental import pallas as pl
from jax.experimental.pallas import tpu as pltpu

F32 = jnp.float32
BF16 = jnp.bfloat16

D_MODEL = 2048
DEPTH = 2
PAGE_SIZE = 128
H_A, KV_A, DH_A = 8, 2, 128
G_A = H_A // KV_A
W_A = H_A * DH_A
CMP_LEN, CMP_STRIDE = 32, 16
SEL_BLOCK, SEL_TOP = 64, 16
WINDOW = 512
FORCE_BONUS = 1.0e4
H_B, DH_B = 8, 128
W_B = H_B * DH_B
H_I, D_I = 16, 128
TOPK_MAX = 256
H_C, D_C = 8, 64
W_C = H_C * 2 * D_C
ALPHA = (2 * DEPTH) ** 0.25
EPS = 1e-5
SPLIT_SIZES = (W_A, 6 * KV_A * DH_A, 3 * H_A, W_A, W_B, 2 * DH_B, H_I * D_I, D_I, H_I, W_B, W_C, W_C, W_C, W_C, 3 * D_MODEL)
SPLIT_OFFS = tuple(int(v) for v in np.cumsum((0,) + SPLIT_SIZES[:-1]))

(P_AQ, P_AKV, P_AG, P_AZ, P_BQ, P_BKV, P_BIQ, P_BIK, P_BIW, P_BZ, P_CQ, P_CK, P_CV, P_CZ, P_MG) = range(15)
LAYOUT_ORDER = (P_AQ, P_AZ, P_BQ, P_BZ, P_CQ, P_CZ, P_MG, P_BIQ, P_CK, P_CV, P_AKV, P_BKV, P_BIK, P_AG, P_BIW)
LANE = 128
N_PROJ = 18432


def _layout_offsets():
    offs, o = {}, 0
    for p in LAYOUT_ORDER:
        offs[p] = o
        o += SPLIT_SIZES[p]
    assert o <= N_PROJ and N_PROJ % LANE == 0
    return offs


OFF = _layout_offsets()
VMEM_LIMIT = 48 * 1024 * 1024


def _alibi_slopes(n):
    return jnp.asarray([2.0 ** (-8.0 * (i + 1) / n) for i in range(n)], F32)


def _proj_kernel(x_ref, w_ref, o_ref, xb_ref):
    @pl.when(pl.program_id(1) == 0)
    def _():
        xb_ref[...] = x_ref[...].astype(BF16)

    o_ref[...] = jnp.dot(xb_ref[...], w_ref[...], preferred_element_type=F32)


def _project(x2d, w_bf16):
    m, k = x2d.shape
    n = w_bf16.shape[1]
    tm = next(t for t in (512, 256, 128, m) if m % t == 0)
    tn = 1024
    assert n % tn == 0
    return pl.pallas_call(
        _proj_kernel,
        grid=(m // tm, n // tn),
        in_specs=[pl.BlockSpec((tm, k), lambda i, j: (i, 0)),
                  pl.BlockSpec((k, tn), lambda i, j: (0, j))],
        out_specs=pl.BlockSpec((tm, tn), lambda i, j: (i, j)),
        out_shape=jax.ShapeDtypeStruct((m, n), F32),
        scratch_shapes=[pltpu.VMEM((tm, k), BF16)],
        compiler_params=pltpu.CompilerParams(dimension_semantics=("parallel", "arbitrary"),
                                             vmem_limit_bytes=VMEM_LIMIT),
        name="proj",
    )(x2d, w_bf16)


def _permute_w_in(w):
    cols = [w[:, SPLIT_OFFS[p]:SPLIT_OFFS[p] + SPLIT_SIZES[p]] for p in LAYOUT_ORDER]
    used = sum(SPLIT_SIZES)
    cols.append(jnp.zeros((w.shape[0], N_PROJ - used), w.dtype))
    return jnp.concatenate(cols, axis=1).astype(BF16)


def _piece(h, p):
    return h[..., OFF[p]:OFF[p] + SPLIT_SIZES[p]]


def _sigmoid(v):
    return 1.0 / (1.0 + jnp.exp(-v))


def _merge_kernel(oa_ref, ob_ref, oc_ref, za_ref, zb_ref, zc_ref, mg_ref, x_ref, wbr_ref, wout_ref, g_ref, b_ref, y_ref):
    m = None
    for n, (o_ref, z_ref) in enumerate(((oa_ref, za_ref), (ob_ref, zb_ref), (oc_ref, zc_ref))):
        z = z_ref[...]
        u = (o_ref[...] * (z * _sigmoid(z))).astype(BF16)
        br = jnp.dot(u, wbr_ref[n], preferred_element_type=F32)
        t = _sigmoid(mg_ref[:, n * D_MODEL:(n + 1) * D_MODEL]) * br
        m = t if m is None else m + t
    y = jnp.dot(m.astype(BF16), wout_ref[...], preferred_element_type=F32)
    r = ALPHA * x_ref[...] + y
    mu = jnp.mean(r, axis=-1, keepdims=True)
    rc = r - mu
    var = jnp.mean(rc * rc, axis=-1, keepdims=True)
    y_ref[...] = rc * lax.rsqrt(var + EPS) * g_ref[...] + b_ref[...]


def _merge(oa, ob, oc, h, x2d, wbr_bf16, wout_bf16, ln_g, ln_b):
    m = x2d.shape[0]
    tm = min(128, m)
    assert m % tm == 0
    w1 = W_A
    zspec = lambda p: pl.BlockSpec((tm, w1), lambda i, c=OFF[p] // w1: (i, c))
    ospec = pl.BlockSpec((tm, w1), lambda i: (i, 0))
    const = lambda shape: pl.BlockSpec(shape, lambda i: (0,) * len(shape), pipeline_mode=pl.Buffered(1))
    return pl.pallas_call(
        _merge_kernel,
        grid=(m // tm,),
        in_specs=[ospec, ospec, ospec, zspec(P_AZ), zspec(P_BZ), zspec(P_CZ),
                  pl.BlockSpec((tm, 3 * D_MODEL), lambda i, c=OFF[P_MG] // (3 * D_MODEL): (i, c)),
                  pl.BlockSpec((tm, D_MODEL), lambda i: (i, 0)),
                  const((3, w1, D_MODEL)), const((D_MODEL, D_MODEL)),
                  const((1, D_MODEL)), const((1, D_MODEL))],
        out_specs=pl.BlockSpec((tm, D_MODEL), lambda i: (i, 0)),
        out_shape=jax.ShapeDtypeStruct((m, D_MODEL), F32),
        compiler_params=pltpu.CompilerParams(dimension_semantics=("parallel",), vmem_limit_bytes=VMEM_LIMIT),
        name="merge",
    )(oa, ob, oc, h, h, h, h, x2d, wbr_bf16, wout_bf16, ln_g.reshape(1, -1), ln_b.reshape(1, -1))


def _dot_nt(a, b):
    return lax.dot_general(a, b, (((1,), (1,)), ((), ())), preferred_element_type=F32)


def _iota(shape, dim):
    return lax.broadcasted_iota(jnp.int32, shape, dim)


def _softmax_rows(s, mask):
    s = jnp.where(mask, s, -jnp.inf)
    m = jnp.max(s, axis=-1, keepdims=True)
    m = jnp.where(m == -jnp.inf, 0.0, m)
    e = jnp.where(mask, jnp.exp(s - m), 0.0)
    return e / jnp.maximum(jnp.sum(e, axis=-1, keepdims=True), 1e-30)


def _lam_init(li):
    return 0.8 - 0.6 * math.exp(-0.3 * li)


def _lam_value(lam_ref, li):
    lf = lam_ref[...]
    a = jnp.sum(lf[0:1] * lf[1:2], axis=-1, keepdims=True)
    b = jnp.sum(lf[2:3] * lf[3:4], axis=-1, keepdims=True)
    return jnp.exp(a) - jnp.exp(b) + _lam_init(li)


def _diff_norm(o, subg, li):
    return o * lax.rsqrt(jnp.mean(o * o, axis=-1, keepdims=True) + EPS) * subg * (1.0 - _lam_init(li))


def _diff_prompt_kernel(slopes_ref, lam_ref, subg_ref, q_ref, k_ref, v_ref, o_ref, *, tq, nq, li):
    hh = pl.program_id(1)
    i = pl.program_id(2)
    slope = slopes_ref[hh]
    lam_val = _lam_value(lam_ref, li)
    scale = D_C ** -0.5

    def body(kmax):
        q = q_ref[...]
        lane = _iota((1, 2 * D_C), 1)
        qlo = jnp.where(lane < D_C, q, 0.0).astype(BF16)
        qhi = jnp.where(lane >= D_C, q, 0.0).astype(BF16)
        k = k_ref[0:kmax, :].astype(BF16)
        v = v_ref[0:kmax, :].astype(BF16)
        dist = (i * tq + _iota((tq, kmax), 0)) - _iota((tq, kmax), 1)
        mask = dist >= 0
        bias = -slope * dist.astype(F32)
        p1 = _softmax_rows(_dot_nt(qlo, k) * scale + bias, mask)
        p2 = _softmax_rows(_dot_nt(qhi, k) * scale + bias, mask)
        a = p1 - lam_val * p2
        o = jnp.dot(a.astype(BF16), v, preferred_element_type=F32)
        o_ref[...] = _diff_norm(o, subg_ref[...], li)

    for j in range(nq):
        pl.when(i == j)(functools.partial(body, (j + 1) * tq))


def _diff_prompt(h2d, B, S, lam, sub_g, li):
    tq = 256 if S % 256 == 0 else 128
    nq = S // tq
    cb = lambda p: OFF[p] // LANE
    smem = pl.BlockSpec(memory_space=pltpu.SMEM)
    return pl.pallas_call(
        functools.partial(_diff_prompt_kernel, tq=tq, nq=nq, li=li),
        grid=(B, H_C, nq),
        in_specs=[smem,
                  pl.BlockSpec((4, D_C), lambda b, h, i: (0, 0)),
                  pl.BlockSpec((1, 2 * D_C), lambda b, h, i: (0, 0)),
                  pl.BlockSpec((tq, LANE), lambda b, h, i: (b * nq + i, cb(P_CQ) + h)),
                  pl.BlockSpec((S, LANE), lambda b, h, i: (b, cb(P_CK) + h)),
                  pl.BlockSpec((S, LANE), lambda b, h, i: (b, cb(P_CV) + h))],
        out_specs=pl.BlockSpec((tq, LANE), lambda b, h, i: (b * nq + i, h)),
        out_shape=jax.ShapeDtypeStruct((B * S, W_C), F32),
        compiler_params=pltpu.CompilerParams(dimension_semantics=("parallel", "parallel", "arbitrary"),
                                             vmem_limit_bytes=VMEM_LIMIT),
        name="diff_prompt",
    )(_alibi_slopes(H_C), lam, sub_g.reshape(1, -1), h2d, h2d, h2d)


def _head_slopes(hrow, n):
    out = jnp.zeros(hrow.shape, F32)
    for h in range(n):
        out = jnp.where(hrow == h, 2.0 ** (-8.0 * (h + 1) / n), out)
    return out


def _diff_sample_kernel(pt_ref, lam_ref, subg_ref, q_ref, new_ref, *rest, pps, n_steps, past, ds, li):
    page_refs = rest[:pps]
    o_ref = rest[pps]
    qf_ref, qb_ref, m_ref, l_ref, acc_ref = rest[pps + 1:]
    step = pl.program_id(1)
    nr = ds * H_C
    scale = D_C ** -0.5
    row = _iota((2 * nr, 1), 0)
    slope = _head_slopes(row % H_C, H_C)
    qpos = past + (row % nr) // H_C

    @pl.when(step == 0)
    def _():
        q = q_ref[0]
        r2 = _iota((2 * nr, W_C), 0)
        lane = _iota((2 * nr, W_C), 1)
        keep = (lane // (2 * D_C) == r2 % H_C) & ((lane % (2 * D_C) >= D_C) == (r2 >= nr))
        qbd = jnp.zeros((2 * nr, W_C), F32)
        for t in range(ds):
            qbd = jnp.where(keep & ((r2 % nr) // H_C == t), q[t:t + 1, :], qbd)
        qf_ref[...] = qbd
        qb_ref[...] = qbd.astype(BF16)
        m_ref[...] = jnp.full(m_ref.shape, -jnp.inf, F32)
        l_ref[...] = jnp.zeros(l_ref.shape, F32)
        acc_ref[...] = jnp.zeros(acc_ref.shape, F32)

    for j, pref in enumerate(page_refs):
        k = pref[0, 0, :, 0:W_C].astype(BF16)
        v = pref[0, 0, :, W_C:2 * W_C].astype(BF16)
        kpos = (step * pps + j) * PAGE_SIZE + _iota((1, PAGE_SIZE), 1)
        dist = qpos - kpos
        s = _dot_nt(qb_ref[...], k) * scale - slope * dist.astype(F32)
        mask = dist >= 0
        s = jnp.where(mask, s, -jnp.inf)
        m_old = m_ref[...]
        m_new = jnp.maximum(m_old, jnp.max(s, axis=-1, keepdims=True))
        m_use = jnp.where(m_new == -jnp.inf, 0.0, m_new)
        alpha = jnp.exp(m_old - m_use)
        e = jnp.where(mask, jnp.exp(s - m_use), 0.0)
        l_ref[...] = alpha * l_ref[...] + jnp.sum(e, axis=-1, keepdims=True)
        acc_ref[...] = alpha * acc_ref[...] + jnp.dot(e.astype(BF16), v, preferred_element_type=F32)
        m_ref[...] = m_new

    @pl.when(step == n_steps - 1)
    def _():
        qf = qf_ref[...]
        new = new_ref[0]
        s_cols = []
        for t in range(ds):
            st = jnp.sum(qf * new[t:t + 1, 0:W_C], axis=-1, keepdims=True) * scale
            dist = qpos - (past + t)
            s_cols.append(jnp.where(dist >= 0, st - slope * dist.astype(F32), -jnp.inf))
        m_old = m_ref[...]
        m_new = m_old
        for st in s_cols:
            m_new = jnp.maximum(m_new, st)
        m_use = jnp.where(m_new == -jnp.inf, 0.0, m_new)
        alpha = jnp.exp(m_old - m_use)
        l = alpha * l_ref[...]
        acc = alpha * acc_ref[...]
        for t, st in enumerate(s_cols):
            e = jnp.where(st == -jnp.inf, 0.0, jnp.exp(st - m_use))
            l = l + e
            acc = acc + e * new[t:t + 1, W_C:2 * W_C]
        o = acc / jnp.maximum(l, 1e-30)
        od = o[0:nr] - _lam_value(lam_ref, li) * o[nr:2 * nr]
        own = _iota((nr, W_C), 1) // (2 * D_C) == _iota((nr, W_C), 0) % H_C
        od = jnp.where(own, od, 0.0)
        ms = jnp.sum(od * od, axis=-1, keepdims=True) * (1.0 / (2 * D_C))
        od = od * lax.rsqrt(ms + EPS) * subg_ref[...] * (1.0 - _lam_init(li))
        o_ref[0] = jnp.sum(od.reshape(ds, H_C, W_C), axis=1)


def _pages_per_step(n_pages):
    return next(t for t in (4, 3, 2, 1) if n_pages % t == 0)


def _diff_sample(q3, new3, cache_diff, page_table, lam, sub_g, li):
    DB, ds, _ = q3.shape
    n_pages = page_table.shape[1]
    pps = _pages_per_step(n_pages)
    n_steps = n_pages // pps
    nr = H_C * ds
    page_spec = lambda j: pl.BlockSpec((1, 1, PAGE_SIZE, 2 * W_C), lambda b, s, pt, j=j: (li, pt[b, s * pps + j], 0, 0))
    grid_spec = pltpu.PrefetchScalarGridSpec(
        num_scalar_prefetch=1,
        grid=(DB, n_steps),
        in_specs=[pl.BlockSpec((4, D_C), lambda b, s, pt: (0, 0)),
                  pl.BlockSpec((1, W_C), lambda b, s, pt: (0, 0)),
                  pl.BlockSpec((1, ds, W_C), lambda b, s, pt: (b, 0, 0)),
                  pl.BlockSpec((1, ds, 2 * W_C), lambda b, s, pt: (b, 0, 0))] + [page_spec(j) for j in range(pps)],
        out_specs=pl.BlockSpec((1, ds, W_C), lambda b, s, pt: (b, 0, 0)),
        scratch_shapes=[pltpu.VMEM((2 * nr, W_C), F32), pltpu.VMEM((2 * nr, W_C), BF16),
                        pltpu.VMEM((2 * nr, 1), F32), pltpu.VMEM((2 * nr, 1), F32), pltpu.VMEM((2 * nr, W_C), F32)],
    )
    return pl.pallas_call(
        functools.partial(_diff_sample_kernel, pps=pps, n_steps=n_steps, past=n_pages * PAGE_SIZE, ds=ds, li=li),
        grid_spec=grid_spec,
        out_shape=jax.ShapeDtypeStruct((DB, ds, W_C), F32),
        compiler_params=pltpu.CompilerParams(dimension_semantics=("parallel", "arbitrary"),
                                             vmem_limit_bytes=VMEM_LIMIT),
        name="diff_sample",
    )(page_table, lam, jnp.tile(sub_g.reshape(1, -1), (1, H_C)), q3, new3, *([cache_diff] * pps))


INT_MIN = -2 ** 31
KEY_NEG_INF = (0xFF800000 ^ 0x7FFFFFFF) - 2 ** 32


def _sort_key(x):
    bits = pltpu.bitcast(x, jnp.int32)
    return jnp.where(bits < 0, bits ^ 0x7FFFFFFF, bits)


def _kth_largest_key(skey, k):
    def step(t, acc):
        cand = acc | lax.shift_left(jnp.int32(1), 31 - t)
        cnt = jnp.sum(jnp.where(skey >= (cand ^ INT_MIN), 1.0, 0.0), axis=-1, keepdims=True)
        return jnp.where(cnt >= k, cand, acc)

    acc = lax.fori_loop(0, 32, step, jnp.zeros((skey.shape[0], 1), jnp.int32))
    return acc ^ INT_MIN


def _topk_select(skey, k, sel_ref, width):
    thr = _kth_largest_key(skey, k)
    ge = skey >= thr
    sel_ref[:, 0:width] = jnp.where(ge, 1.0, 0.0)
    cnt_ge = jnp.sum(jnp.where(ge, 1.0, 0.0), axis=-1, keepdims=True)
    tied = jnp.where((cnt_ge > k) & (thr != KEY_NEG_INF), 1.0, 0.0)

    @pl.when(jnp.max(tied) > 0.0)
    def _():
        gt = skey > thr
        need = k - jnp.sum(jnp.where(gt, 1.0, 0.0), axis=-1, keepdims=True)
        ch = 256 if width % 256 == 0 else LANE
        tri = jnp.where(_iota((ch, ch), 0) <= _iota((ch, ch), 1), 1.0, 0.0).astype(BF16)
        run = jnp.zeros((skey.shape[0], 1), F32)
        for c in range(0, width, ch):
            eq = jnp.where(skey[:, c:c + ch] == thr, 1.0, 0.0)
            pre = jnp.dot(eq.astype(BF16), tri, preferred_element_type=F32) + run
            keep = gt[:, c:c + ch] | ((eq > 0.0) & (pre <= need))
            sel_ref[:, c:c + ch] = jnp.where(keep, 1.0, 0.0)
            run = run + jnp.sum(eq, axis=-1, keepdims=True)


def _dsa_prompt_kernel(slopes_ref, iq_ref, misc_ref, q_ref, k_ref, v_ref, ik_ref, o_ref, sel_ref, *, tq, nq, k_sel):
    i = pl.program_id(1)

    def body(kmax):
        ik = ik_ref[0:kmax, :].astype(BF16)
        score = jnp.zeros((tq, kmax), F32)
        for h in range(H_I):
            iqh = iq_ref[:, h * D_I:(h + 1) * D_I].astype(BF16)
            rel = jnp.maximum(_dot_nt(iqh, ik) * D_I ** -0.5, 0.0)
            w = misc_ref[:, 3 * H_A + h:3 * H_A + h + 1] * H_I ** -0.5
            score = score + w * rel
        dist = (i * tq + _iota((tq, kmax), 0)) - _iota((tq, kmax), 1)
        causal = dist >= 0
        skey = _sort_key(jnp.where(causal, score, -jnp.inf))
        _topk_select(skey, k_sel, sel_ref, kmax)
        mask = (sel_ref[:, 0:kmax] > 0.0) & causal
        distf = dist.astype(F32)
        k = k_ref[0:kmax, :].astype(BF16)
        v = v_ref[0:kmax, :].astype(BF16)
        for h in range(H_B):
            qh = q_ref[:, h * DH_B:(h + 1) * DH_B].astype(BF16)
            s = _dot_nt(qh, k) * DH_B ** -0.5 - slopes_ref[h] * distf
            p = _softmax_rows(s, mask)
            o_ref[:, h * DH_B:(h + 1) * DH_B] = jnp.dot(p.astype(BF16), v, preferred_element_type=F32)

    for j in range(nq):
        pl.when(i == j)(functools.partial(body, (j + 1) * tq))


def _dsa_prompt(h2d, B, S):
    tq = 256 if S % 256 == 0 else 128
    nq = S // tq
    k_sel = min(TOPK_MAX, S // 4)
    cb = lambda p, w=LANE: OFF[p] // w
    kspec = lambda c: pl.BlockSpec((S, LANE), lambda b, i, c=c: (b, c))
    return pl.pallas_call(
        functools.partial(_dsa_prompt_kernel, tq=tq, nq=nq, k_sel=k_sel),
        grid=(B, nq),
        in_specs=[pl.BlockSpec(memory_space=pltpu.SMEM),
                  pl.BlockSpec((tq, H_I * D_I), lambda b, i: (b * nq + i, cb(P_BIQ, H_I * D_I))),
                  pl.BlockSpec((tq, LANE), lambda b, i: (b * nq + i, cb(P_AG))),
                  pl.BlockSpec((tq, W_B), lambda b, i: (b * nq + i, cb(P_BQ, W_B))),
                  kspec(cb(P_BKV)), kspec(cb(P_BKV) + 1), kspec(cb(P_BIK))],
        out_specs=pl.BlockSpec((tq, W_B), lambda b, i: (b * nq + i, 0)),
        out_shape=jax.ShapeDtypeStruct((B * S, W_B), F32),
        scratch_shapes=[pltpu.VMEM((tq, S), F32)],
        compiler_params=pltpu.CompilerParams(dimension_semantics=("parallel", "arbitrary"),
                                             vmem_limit_bytes=VMEM_LIMIT),
        name="dsa_prompt",
    )(_alibi_slopes(H_B), h2d, h2d, h2d, h2d, h2d, h2d)


def _compress_chunks(a, pos_ref, w_ref, c):
    n_ch = a.shape[0]
    t0 = jnp.dot((a + pos_ref[c, 0]).astype(BF16), w_ref[c, 0], preferred_element_type=F32)
    t1 = jnp.dot((a + pos_ref[c, 1]).astype(BF16), w_ref[c, 1], preferred_element_type=F32)
    return t0 + pltpu.roll(t1, n_ch - 1, 0)


def _cmp_prompt_kernel(x_ref, pos_ref, w_ref, o_ref, *, n_ch):
    a = jnp.concatenate([x_ref[pl.ds(l, n_ch, stride=CMP_STRIDE), :] for l in range(CMP_STRIDE)], axis=1)
    o_ref[0, 0] = _compress_chunks(a, pos_ref, w_ref, 0)


def _cmp_weights(w_cmp, cmp_pos):
    R = CMP_LEN // CMP_STRIDE
    return (w_cmp.reshape(2, R, CMP_STRIDE * DH_A, DH_A).astype(BF16),
            cmp_pos.reshape(2, R, 1, CMP_STRIDE * DH_A))


def _cmp_prompt(h2d, B, S, w_cmp, cmp_pos):
    assert S % (8 * CMP_STRIDE) == 0 and CMP_LEN == 2 * CMP_STRIDE
    n_ch = S // CMP_STRIDE
    w4, p4 = _cmp_weights(w_cmp, cmp_pos)
    kd = CMP_STRIDE * DH_A
    return pl.pallas_call(
        functools.partial(_cmp_prompt_kernel, n_ch=n_ch),
        grid=(B, 2 * KV_A),
        in_specs=[pl.BlockSpec((S, LANE), lambda b, pc: (b, OFF[P_AKV] // LANE + pc)),
                  pl.BlockSpec((1, 2, 1, kd), lambda b, pc: (pc // KV_A, 0, 0, 0)),
                  pl.BlockSpec((1, 2, kd, DH_A), lambda b, pc: (pc // KV_A, 0, 0, 0))],
        out_specs=pl.BlockSpec((1, 1, n_ch, DH_A), lambda b, pc: (b, pc, 0, 0)),
        out_shape=jax.ShapeDtypeStruct((B, 2 * KV_A, n_ch, DH_A), F32),
        compiler_params=pltpu.CompilerParams(dimension_semantics=("parallel", "arbitrary"),
                                             vmem_limit_bytes=VMEM_LIMIT),
        name="cmp_prompt",
    )(h2d, p4, w4)


def _split3(x):
    hi = x.astype(BF16)
    r1 = x - hi.astype(F32)
    mid = r1.astype(BF16)
    lo = (r1 - mid.astype(F32)).astype(BF16)
    return hi, mid, lo


def _block_select(psum, qpos, n_cmp, n_sel):
    rows, ncp = psum.shape
    jj = _iota((ncp, LANE), 0) * CMP_STRIDE
    ii = _iota((ncp, LANE), 1) * SEL_BLOCK
    ov = jnp.maximum(jnp.minimum(jj + CMP_LEN, ii + SEL_BLOCK) - jnp.maximum(jj, ii), 0)
    ovm = (ov.astype(F32) * (1.0 / CMP_LEN)).astype(BF16)
    imp = None
    for part in _split3(psum):
        t = jnp.dot(part, ovm, preferred_element_type=F32)
        imp = t if imp is None else imp + t
    ib = _iota((rows, LANE), 1)
    cur = qpos // SEL_BLOCK
    valid = (ib * SEL_BLOCK <= qpos) & (ib < n_sel)
    forced = (ib == 0) | (ib == cur) | (ib == cur - 1)
    imp = jnp.where(valid, imp + jnp.where(forced, FORCE_BONUS, 0.0), -jnp.inf)
    rank = jnp.zeros((rows, LANE), F32)
    for j in range(n_sel):
        col = imp[:, j:j + 1]
        ahead = (col > imp) | ((col == imp) & (ib > j))
        rank = rank + jnp.where(ahead, 1.0, 0.0)
    return jnp.where((rank < min(SEL_TOP, n_sel)) & (ib < n_sel), 1.0, 0.0)


def _cmpsel_prompt_kernel(slopes_ref, q_ref, kc_ref, ocmp_ref, sel_ref, *, tq, n_ch, n_sel):
    c = pl.program_id(1)
    i = pl.program_id(2)
    scale = DH_A ** -0.5
    n_cmp = n_ch - 1
    kck = kc_ref[0, c].astype(BF16)
    kcv = kc_ref[0, KV_A + c].astype(BF16)
    qpos = i * tq + _iota((tq, 1), 0)
    col = _iota((1, n_ch), 1)
    dist = qpos - (col * CMP_STRIDE + (CMP_LEN - 1))
    mask = (dist >= 0) & (col < n_cmp)
    distf = dist.astype(F32)
    psum = jnp.zeros((tq, n_ch), F32)
    for g in range(G_A):
        qg = q_ref[:, g * DH_A:(g + 1) * DH_A].astype(BF16)
        s = _dot_nt(qg, kck) * scale - slopes_ref[c * G_A + g] * distf
        p = _softmax_rows(s, mask)
        ocmp_ref[:, g * DH_A:(g + 1) * DH_A] = jnp.dot(p.astype(BF16), kcv, preferred_element_type=F32)
        psum = psum + p
    sel_ref[0, 0] = _block_select(psum, qpos, n_cmp, n_sel)


def _cmpsel_prompt(h2d, kc, B, S):
    tq = 256 if S % 256 == 0 else 128
    nq = S // tq
    n_ch = S // CMP_STRIDE
    n_sel = S // SEL_BLOCK
    gw = G_A * DH_A
    return pl.pallas_call(
        functools.partial(_cmpsel_prompt_kernel, tq=tq, n_ch=n_ch, n_sel=n_sel),
        grid=(B, KV_A, nq),
        in_specs=[pl.BlockSpec(memory_space=pltpu.SMEM),
                  pl.BlockSpec((tq, gw), lambda b, c, i: (b * nq + i, OFF[P_AQ] // gw + c)),
                  pl.BlockSpec((1, 2 * KV_A, n_ch, DH_A), lambda b, c, i: (b, 0, 0, 0))],
        out_specs=[pl.BlockSpec((tq, gw), lambda b, c, i: (b * nq + i, c)),
                   pl.BlockSpec((1, 1, tq, LANE), lambda b, c, i: (b, c, i, 0))],
        out_shape=[jax.ShapeDtypeStruct((B * S, W_A), F32), jax.ShapeDtypeStruct((B, KV_A, S, LANE), F32)],
        compiler_params=pltpu.CompilerParams(dimension_semantics=("parallel", "parallel", "arbitrary"),
                                             vmem_limit_bytes=VMEM_LIMIT),
        name="cmpsel_prompt",
    )(_alibi_slopes(H_A), h2d, kc)


def _gate(misc, lane_idx, col):
    return jnp.sum(jnp.where(lane_idx == col, _sigmoid(misc), 0.0), axis=-1, keepdims=True)


def _slcwin_prompt_kernel(slopes_ref, q_ref, misc_ref, ocmp_ref, sel_ref, sk_ref, sv_ref, wk_ref, wv_ref, o_ref, *, tq, nq):
    hh = pl.program_id(1)
    i = pl.program_id(2)
    scale = DH_A ** -0.5
    slope = slopes_ref[hh]

    def body(j):
        q0 = j * tq
        kmax = q0 + tq
        q = q_ref[...].astype(BF16)
        qpos = q0 + _iota((tq, 1), 0)
        ex = jnp.where(_iota((LANE, kmax), 1) // SEL_BLOCK == _iota((LANE, kmax), 0), 1.0, 0.0).astype(BF16)
        keym = jnp.dot(sel_ref[0, 0].astype(BF16), ex, preferred_element_type=F32)
        dist = qpos - _iota((1, kmax), 1)
        mask = (keym > 0.5) & (dist >= 0)
        s = _dot_nt(q, sk_ref[0:kmax, :].astype(BF16)) * scale - slope * dist.astype(F32)
        o_slc = jnp.dot(_softmax_rows(s, mask).astype(BF16), sv_ref[0:kmax, :].astype(BF16), preferred_element_type=F32)
        ws = max(0, q0 - WINDOW)
        dw = qpos - (ws + _iota((1, kmax - ws), 1))
        mw = (dw >= 0) & (dw <= WINDOW)
        sw = _dot_nt(q, wk_ref[ws:kmax, :].astype(BF16)) * scale - slope * dw.astype(F32)
        o_win = jnp.dot(_softmax_rows(sw, mw).astype(BF16), wv_ref[ws:kmax, :].astype(BF16), preferred_element_type=F32)
        misc = misc_ref[...]
        lane = _iota((tq, LANE), 1)
        o_ref[...] = (_gate(misc, lane, 3 * hh) * ocmp_ref[...] + _gate(misc, lane, 3 * hh + 1) * o_slc
                      + _gate(misc, lane, 3 * hh + 2) * o_win)

    for j in range(nq):
        pl.when(i == j)(functools.partial(body, j))


def _slcwin_prompt(h2d, ocmp, sel, B, S):
    tq = 256 if S % 256 == 0 else 128
    nq = S // tq
    akv = OFF[P_AKV] // LANE
    row = lambda w, cfn: pl.BlockSpec((tq, w), lambda b, h, i: (b * nq + i, cfn(h)))
    kvs = lambda piece: pl.BlockSpec((S, LANE), lambda b, h, i: (b, akv + piece * KV_A + h // G_A))
    return pl.pallas_call(
        functools.partial(_slcwin_prompt_kernel, tq=tq, nq=nq),
        grid=(B, H_A, nq),
        in_specs=[pl.BlockSpec(memory_space=pltpu.SMEM),
                  row(LANE, lambda h: OFF[P_AQ] // LANE + h),
                  row(LANE, lambda h: OFF[P_AG] // LANE),
                  row(LANE, lambda h: h),
                  pl.BlockSpec((1, 1, tq, LANE), lambda b, h, i: (b, h // G_A, i, 0)),
                  kvs(2), kvs(3), kvs(4), kvs(5)],
        out_specs=row(LANE, lambda h: h),
        out_shape=jax.ShapeDtypeStruct((B * S, W_A), F32),
        compiler_params=pltpu.CompilerParams(dimension_semantics=("parallel", "parallel", "arbitrary"),
                                             vmem_limit_bytes=VMEM_LIMIT),
        name="slcwin_prompt",
    )(_alibi_slopes(H_A), h2d, h2d, ocmp, sel, h2d, h2d, h2d, h2d)


def _nsa_prompt(h2d, B, S, w_cmp, cmp_pos):
    kc = _cmp_prompt(h2d, B, S, w_cmp, cmp_pos)
    ocmp, sel = _cmpsel_prompt(h2d, kc, B, S)
    return _slcwin_prompt(h2d, ocmp, sel, B, S)


def _dsa_sample_kernel(pt_ref, iq_ref, iw_ref, q_ref, new_ref, *rest, n_pages, ds, k_sel):
    page_refs = rest[:n_pages]
    o_ref = rest[n_pages]
    score_ref, sel_ref, s_ref = rest[n_pages + 1:]
    past = n_pages * PAGE_SIZE
    width = past + LANE
    lane = _iota((1, LANE), 1)
    new = new_ref[0]
    iqf = iq_ref[0]
    iq = iqf.astype(BF16)
    iw = iw_ref[0] * H_I ** -0.5
    score_ref[...] = jnp.full(score_ref.shape, -jnp.inf, F32)
    for p, pref in enumerate(page_refs):
        ik = pref[0, 0, :, 2 * DH_B:3 * DH_B].astype(BF16)
        rel = jnp.maximum(_dot_nt(iq, ik) * D_I ** -0.5, 0.0) * iw
        score_ref[0:ds, p * PAGE_SIZE:(p + 1) * PAGE_SIZE] = jnp.sum(rel.reshape(ds, H_I, PAGE_SIZE), axis=1)
    qi = _iota((ds, 1), 0)
    tile = jnp.full((ds, LANE), -jnp.inf, F32)
    for t in range(ds):
        rel = jnp.maximum(jnp.sum(iqf * new[t:t + 1, 2 * DH_B:3 * DH_B], axis=-1, keepdims=True) * D_I ** -0.5, 0.0) * iw
        sc = jnp.sum(rel.reshape(ds, H_I, 1), axis=1)
        tile = jnp.where((lane == t) & (qi >= t), sc, tile)
    score_ref[0:ds, past:width] = tile
    _topk_select(_sort_key(score_ref[...]), k_sel, sel_ref, width)
    nr = ds * H_B
    row = _iota((nr, 1), 0)
    slope = _head_slopes(row % H_B, H_B)
    qpos = past + row // H_B
    qf = q_ref[0]
    q = qf.astype(BF16)
    scale = DH_B ** -0.5

    def sel_rows(c0):
        s = sel_ref[0:ds, c0:c0 + LANE]
        return jnp.broadcast_to(s[:, None, :], (ds, H_B, LANE)).reshape(nr, LANE) > 0.0

    for p, pref in enumerate(page_refs):
        k = pref[0, 0, :, 0:DH_B].astype(BF16)
        dist = qpos - (p * PAGE_SIZE + lane)
        s = _dot_nt(q, k) * scale - slope * dist.astype(F32)
        s_ref[:, p * PAGE_SIZE:(p + 1) * PAGE_SIZE] = jnp.where(sel_rows(p * PAGE_SIZE) & (dist >= 0), s, -jnp.inf)
    tile = jnp.full((nr, LANE), -jnp.inf, F32)
    for t in range(ds):
        dist = qpos - (past + t)
        st = jnp.sum(qf * new[t:t + 1, 0:DH_B], axis=-1, keepdims=True) * scale - slope * dist.astype(F32)
        tile = jnp.where((lane == t) & (dist >= 0), st, tile)
    s_ref[:, past:width] = jnp.where(sel_rows(past), tile, -jnp.inf)
    s = s_ref[...]
    m = jnp.max(s, axis=-1, keepdims=True)
    m = jnp.where(m == -jnp.inf, 0.0, m)
    e = jnp.exp(s - m)
    pr = e / jnp.maximum(jnp.sum(e, axis=-1, keepdims=True), 1e-30)
    o = jnp.zeros((nr, DH_B), F32)
    for p, pref in enumerate(page_refs):
        v = pref[0, 0, :, DH_B:2 * DH_B].astype(BF16)
        o = o + jnp.dot(pr[:, p * PAGE_SIZE:(p + 1) * PAGE_SIZE].astype(BF16), v, preferred_element_type=F32)
    for t in range(ds):
        o = o + pr[:, past + t:past + t + 1] * new[t:t + 1, DH_B:2 * DH_B]
    o_ref[0] = o


def _dsa_sample(hs, cache_dsa, page_table, li):
    DB, ds, _ = hs.shape
    n_pages = page_table.shape[1]
    past = n_pages * PAGE_SIZE
    width = past + LANE
    k_sel = min(TOPK_MAX, (past + ds) // 4)
    iq3 = _piece(hs, P_BIQ).reshape(DB, ds * H_I, D_I)
    iw3 = _piece(hs, P_BIW).reshape(DB, ds * H_I, 1)
    q3 = _piece(hs, P_BQ).reshape(DB, ds * H_B, DH_B)
    new3 = hs[..., OFF[P_BKV]:OFF[P_BIK] + D_I]
    per_b = lambda shape: pl.BlockSpec((1,) + shape, lambda b, pt: (b, 0, 0))
    page_spec = lambda p: pl.BlockSpec((1, 1, PAGE_SIZE, 3 * DH_B), lambda b, pt, p=p: (li, pt[b, p], 0, 0))
    grid_spec = pltpu.PrefetchScalarGridSpec(
        num_scalar_prefetch=1,
        grid=(DB,),
        in_specs=[per_b((ds * H_I, D_I)), per_b((ds * H_I, 1)), per_b((ds * H_B, DH_B)), per_b((ds, 3 * DH_B))]
        + [page_spec(p) for p in range(n_pages)],
        out_specs=per_b((ds * H_B, DH_B)),
        scratch_shapes=[pltpu.VMEM((8, width), F32), pltpu.VMEM((8, width), F32), pltpu.VMEM((ds * H_B, width), F32)],
    )
    o = pl.pallas_call(
        functools.partial(_dsa_sample_kernel, n_pages=n_pages, ds=ds, k_sel=k_sel),
        grid_spec=grid_spec,
        out_shape=jax.ShapeDtypeStruct((DB, ds * H_B, DH_B), F32),
        compiler_params=pltpu.CompilerParams(dimension_semantics=("parallel",), vmem_limit_bytes=VMEM_LIMIT),
        name="dsa_sample",
    )(page_table, iq3, iw3, q3, new3, *([cache_dsa] * n_pages))
    return o.reshape(DB, ds, W_B)


def _nsa_sample_kernel(pt_ref, q_ref, gate_ref, new_ref, wnew_ref, st_ref, pos_ref, w_ref, *rest, n_pages, ds, wbuf):
    page_refs = rest[:n_pages]
    o_ref = rest[n_pages]
    s_ref = rest[n_pages + 1]
    x_refs = rest[n_pages + 2:]
    past = n_pages * PAGE_SIZE
    width = past + LANE
    n_ch = past // CMP_STRIDE
    n_sel = -(-(past + ds) // SEL_BLOCK)
    nr = G_A * ds * KV_A
    scale = DH_A ** -0.5
    row = _iota((nr, 1), 0)
    grp = row % KV_A
    qpos = past + (row % (ds * KV_A)) // KV_A
    slope = _head_slopes(grp * G_A + row // (ds * KV_A), H_A)
    lane = _iota((1, LANE), 1)
    qf = q_ref[0]
    q = qf.astype(BF16)
    new = new_ref[0]

    def by_group(fn):
        out = fn(0)
        for c in range(1, KV_A):
            out = jnp.where(grp == c, fn(c), out)
        return out

    kc = []
    for pc, x_ref in enumerate(x_refs):
        for p, pref in enumerate(page_refs):
            x_ref[p * PAGE_SIZE:(p + 1) * PAGE_SIZE, :] = pref[0, 0, :, pc * DH_A:(pc + 1) * DH_A]
        a = jnp.concatenate([x_ref[pl.ds(l, n_ch, stride=CMP_STRIDE), :] for l in range(CMP_STRIDE)], axis=1)
        kc.append(_compress_chunks(a, pos_ref, w_ref, pc // KV_A).astype(BF16))
    col = _iota((1, n_ch), 1)
    dist_c = qpos - (col * CMP_STRIDE + (CMP_LEN - 1))
    mask_c = (dist_c >= 0) & (col < n_ch - 1)
    s = by_group(lambda c: _dot_nt(q, kc[c])) * scale - slope * dist_c.astype(F32)
    p_c = _softmax_rows(s, mask_c)
    p_cb = p_c.astype(BF16)
    o_cmp = by_group(lambda c: jnp.dot(p_cb, kc[KV_A + c], preferred_element_type=F32))
    rq = ds * KV_A
    psum = p_c[0:rq]
    for g in range(1, G_A):
        psum = psum + p_c[g * rq:(g + 1) * rq]
    sel8 = _block_select(psum, qpos[0:rq], n_ch - 1, n_sel)
    sel = jnp.concatenate([sel8] * G_A, axis=0)

    bpp = PAGE_SIZE // SEL_BLOCK
    for p, pref in enumerate(page_refs):
        keym = jnp.zeros((nr, PAGE_SIZE), F32)
        for t in range(bpp):
            keym = jnp.where(lane // SEL_BLOCK == t, sel[:, p * bpp + t:p * bpp + t + 1], keym)
        dist = qpos - (p * PAGE_SIZE + lane)
        sp = by_group(lambda c: _dot_nt(q, pref[0, 0, :, (2 * KV_A + c) * DH_A:(2 * KV_A + c + 1) * DH_A].astype(BF16)))
        sp = sp * scale - slope * dist.astype(F32)
        s_ref[:, p * PAGE_SIZE:(p + 1) * PAGE_SIZE] = jnp.where((keym > 0.5) & (dist >= 0), sp, -jnp.inf)
    tile = jnp.full((nr, LANE), -jnp.inf, F32)
    for t in range(ds):
        dist = qpos - (past + t)
        kt = by_group(lambda c: jnp.broadcast_to(new[t:t + 1, (2 * KV_A + c) * DH_A:(2 * KV_A + c + 1) * DH_A], (nr, DH_A)))
        st = jnp.sum(qf * kt, axis=-1, keepdims=True) * scale - slope * dist.astype(F32)
        tile = jnp.where((lane == t) & (dist >= 0), st, tile)
    s_ref[:, past:width] = jnp.where(sel[:, n_pages * bpp:n_pages * bpp + 1] > 0.5, tile, -jnp.inf)
    s = s_ref[...]
    m = jnp.max(s, axis=-1, keepdims=True)
    m = jnp.where(m == -jnp.inf, 0.0, m)
    e = jnp.exp(s - m)
    pr = e / jnp.maximum(jnp.sum(e, axis=-1, keepdims=True), 1e-30)
    o_slc = jnp.zeros((nr, DH_A), F32)
    for p, pref in enumerate(page_refs):
        pb = pr[:, p * PAGE_SIZE:(p + 1) * PAGE_SIZE].astype(BF16)
        o_slc = o_slc + by_group(lambda c: jnp.dot(
            pb, pref[0, 0, :, (3 * KV_A + c) * DH_A:(3 * KV_A + c + 1) * DH_A].astype(BF16), preferred_element_type=F32))
    for t in range(ds):
        vt = by_group(lambda c: jnp.broadcast_to(new[t:t + 1, (3 * KV_A + c) * DH_A:(3 * KV_A + c + 1) * DH_A], (nr, DH_A)))
        o_slc = o_slc + pr[:, past + t:past + t + 1] * vt

    wnew = wnew_ref[0]
    dw = qpos - (past - wbuf + _iota((1, wbuf), 1))
    sw = by_group(lambda c: _dot_nt(q, st_ref[0, 0, :, c * DH_A:(c + 1) * DH_A].astype(BF16)))
    sw = jnp.where((dw >= 0) & (dw <= WINDOW), sw * scale - slope * dw.astype(F32), -jnp.inf)
    tile = jnp.full((nr, LANE), -jnp.inf, F32)
    for t in range(ds):
        dist = qpos - (past + t)
        kt = by_group(lambda c: jnp.broadcast_to(wnew[t:t + 1, c * DH_A:(c + 1) * DH_A], (nr, DH_A)))
        st = jnp.sum(qf * kt, axis=-1, keepdims=True) * scale - slope * dist.astype(F32)
        tile = jnp.where((lane == t) & (dist >= 0) & (dist <= WINDOW), st, tile)
    m = jnp.maximum(jnp.max(sw, axis=-1, keepdims=True), jnp.max(tile, axis=-1, keepdims=True))
    m = jnp.where(m == -jnp.inf, 0.0, m)
    ew = jnp.exp(sw - m)
    et = jnp.exp(tile - m)
    den = jnp.maximum(jnp.sum(ew, axis=-1, keepdims=True) + jnp.sum(et, axis=-1, keepdims=True), 1e-30)
    pw = (ew / den).astype(BF16)
    o_win = by_group(lambda c: jnp.dot(pw, st_ref[0, 0, :, (KV_A + c) * DH_A:(KV_A + c + 1) * DH_A].astype(BF16),
                                       preferred_element_type=F32))
    pt = et / den
    for t in range(ds):
        vt = by_group(lambda c: jnp.broadcast_to(wnew[t:t + 1, (KV_A + c) * DH_A:(KV_A + c + 1) * DH_A], (nr, DH_A)))
        o_win = o_win + pt[:, t:t + 1] * vt

    g = _sigmoid(gate_ref[0])
    o_ref[0] = g[:, 0:1] * o_cmp + g[:, 1:2] * o_slc + g[:, 2:3] * o_win


def _nsa_sample(hs, win_n, cache_nsa, state_win, page_table, w_cmp, cmp_pos, li):
    DB, ds, _ = hs.shape
    n_pages = page_table.shape[1]
    past = n_pages * PAGE_SIZE
    wbuf = state_win.shape[2]
    assert ds * KV_A == 8 and CMP_LEN == 2 * CMP_STRIDE
    nr = G_A * ds * KV_A
    regroup = lambda a: jnp.transpose(a.reshape(DB, ds, KV_A, G_A, -1), (0, 3, 1, 2, 4)).reshape(DB, nr, -1)
    q3 = regroup(_piece(hs, P_AQ))
    g3 = regroup(_piece(hs, P_AG))
    new3 = hs[..., OFF[P_AKV]:OFF[P_AKV] + 4 * KV_A * DH_A]
    wnew3 = win_n.reshape(DB, ds, 2 * KV_A * DH_A)
    w4, p4 = _cmp_weights(w_cmp, cmp_pos)
    per_b = lambda shape: pl.BlockSpec((1,) + shape, lambda b, pt: (b, 0, 0))
    full = lambda a: pl.BlockSpec(a.shape, lambda b, pt: (0,) * a.ndim)
    page_spec = lambda p: pl.BlockSpec((1, 1, PAGE_SIZE, 4 * KV_A * DH_A), lambda b, pt, p=p: (li, pt[b, p], 0, 0))
    grid_spec = pltpu.PrefetchScalarGridSpec(
        num_scalar_prefetch=1,
        grid=(DB,),
        in_specs=[per_b((nr, DH_A)), per_b((nr, 3)), per_b((ds, 4 * KV_A * DH_A)), per_b((ds, 2 * KV_A * DH_A)),
                  pl.BlockSpec((1, 1, wbuf, 2 * KV_A * DH_A), lambda b, pt: (li, b, 0, 0)), full(p4), full(w4)]
        + [page_spec(p) for p in range(n_pages)],
        out_specs=per_b((nr, DH_A)),
        scratch_shapes=[pltpu.VMEM((nr, past + LANE), F32)] + [pltpu.VMEM((past, DH_A), F32)] * (2 * KV_A),
    )
    o = pl.pallas_call(
        functools.partial(_nsa_sample_kernel, n_pages=n_pages, ds=ds, wbuf=wbuf),
        grid_spec=grid_spec,
        out_shape=jax.ShapeDtypeStruct((DB, nr, DH_A), F32),
        compiler_params=pltpu.CompilerParams(dimension_semantics=("parallel",), vmem_limit_bytes=VMEM_LIMIT),
        name="nsa_sample",
    )(page_table, q3, g3, new3, wnew3, state_win, p4, w4, *([cache_nsa] * n_pages))
    return jnp.transpose(o.reshape(DB, G_A, ds, KV_A, DH_A), (0, 2, 3, 1, 4)).reshape(DB, ds, W_A)


def _new_rows(h, B, S):
    a_kv = _piece(h, P_AKV).reshape(B, S, 6, KV_A, DH_A)
    dsa_rows = h[..., OFF[P_BKV]:OFF[P_BIK] + D_I].reshape(B, S, 3, DH_B)
    diff_rows = h[..., OFF[P_CK]:OFF[P_CV] + W_C].reshape(B, S, 2, H_C, 2 * D_C)
    return a_kv[:, :, :4], a_kv[:, :, 4:], dsa_rows, diff_rows


def _mix(x, h, o_a, o_b, o_c, wbr, wout, ln_g, ln_b):
    B, S, _ = x.shape
    m = B * S
    y = _merge(o_a.reshape(m, W_A), o_b.reshape(m, W_B), o_c.reshape(m, W_C), h.reshape(m, N_PROJ),
               x.reshape(m, D_MODEL), wbr, wout, ln_g, ln_b)
    return y.reshape(B, S, D_MODEL)


def kernel(x_prompt, x_sample, cache_nsa, cache_dsa, cache_diff, state_nsa_win, page_table,
           w_in, w_cmp, cmp_pos, lam, sub_g, w_branch, w_out, ln_g, ln_b):
    B, S, _ = x_prompt.shape
    DB, DS, _ = x_sample.shape
    wbuf = state_nsa_win.shape[2]
    assert wbuf <= page_table.shape[1] * PAGE_SIZE and DS <= CMP_STRIDE
    flat = lambda a: a.reshape(a.shape[:3] + (-1,))
    nsa_pages, dsa_pages, diff_pages, win_state = flat(cache_nsa), flat(cache_dsa), flat(cache_diff), flat(state_nsa_win)
    xp, xs = x_prompt, x_sample
    outs = [[] for _ in range(8)]
    for li in range(DEPTH):
        w_l = _permute_w_in(w_in[li])
        wbr = w_branch[li].astype(BF16)
        wout = w_out[li].astype(BF16)
        hp = _project(xp.reshape(B * S, D_MODEL), w_l)
        oa_p = _nsa_prompt(hp, B, S, w_cmp[li], cmp_pos[li])
        ob_p = _dsa_prompt(hp, B, S)
        oc_p = _diff_prompt(hp, B, S, lam[li], sub_g[li], li)
        hp = hp.reshape(B, S, N_PROJ)
        nsa_r, win_r, dsa_r, diff_r = _new_rows(hp, B, S)
        xp = _mix(xp, hp, oa_p, ob_p, oc_p, wbr, wout, ln_g[li], ln_b[li])
        hs = _project(xs.reshape(DB * DS, D_MODEL), w_l).reshape(DB, DS, N_PROJ)
        nsa_n, win_n, dsa_n, diff_n = _new_rows(hs, DB, DS)
        oa_s = _nsa_sample(hs, win_n, nsa_pages, win_state, page_table, w_cmp[li], cmp_pos[li], li)
        ob_s = _dsa_sample(hs, dsa_pages, page_table, li)
        oc_s = _diff_sample(_piece(hs, P_CQ), hs[..., OFF[P_CK]:OFF[P_CV] + W_C], diff_pages, page_table,
                            lam[li], sub_g[li], li)
        win_hist = jnp.concatenate([state_nsa_win[li], win_n], axis=1)
        xs = _mix(xs, hs, oa_s, ob_s, oc_s, wbr, wout, ln_g[li], ln_b[li])
        for lst, v in zip(outs, (nsa_r, nsa_n, dsa_r, dsa_n, diff_r, diff_n,
                                 win_r[:, S - min(WINDOW, S):], win_hist[:, win_hist.shape[1] - wbuf:])):
            lst.append(v)
    return (xp, xs) + tuple(jnp.stack(l) for l in outs)
```

```python
import functools
import math

import jax
import jax.numpy as jnp
import numpy as np
from jax import lax
from jax.experimental import pallas as pl
from jax.experimental.pallas import tpu as pltpu

F32 = jnp.float32
BF16 = jnp.bfloat16

D_MODEL = 2048
DEPTH = 2
PAGE_SIZE = 128
H_A, KV_A, DH_A = 8, 2, 128
G_A = H_A // KV_A
W_A = H_A * DH_A
CMP_LEN, CMP_STRIDE = 32, 16
SEL_BLOCK, SEL_TOP = 64, 16
WINDOW = 512
FORCE_BONUS = 1.0e4
H_B, DH_B = 8, 128
W_B = H_B * DH_B
H_I, D_I = 16, 128
TOPK_MAX = 256
H_C, D_C = 8, 64
W_C = H_C * 2 * D_C
ALPHA = (2 * DEPTH) ** 0.25
EPS = 1e-5
SPLIT_SIZES = (W_A, 6 * KV_A * DH_A, 3 * H_A, W_A, W_B, 2 * DH_B, H_I * D_I, D_I, H_I, W_B, W_C, W_C, W_C, W_C, 3 * D_MODEL)
SPLIT_OFFS = tuple(int(v) for v in np.cumsum((0,) + SPLIT_SIZES[:-1]))

(P_AQ, P_AKV, P_AG, P_AZ, P_BQ, P_BKV, P_BIQ, P_BIK, P_BIW, P_BZ, P_CQ, P_CK, P_CV, P_CZ, P_MG) = range(15)
LAYOUT_ORDER = (P_AQ, P_AZ, P_BQ, P_BZ, P_CQ, P_CZ, P_MG, P_BIQ, P_CK, P_CV, P_AKV, P_BKV, P_BIK, P_AG, P_BIW)
LANE = 128
N_PROJ = 18432


def _layout_offsets():
    offs, o = {}, 0
    for p in LAYOUT_ORDER:
        offs[p] = o
        o += SPLIT_SIZES[p]
    assert o <= N_PROJ and N_PROJ % LANE == 0
    return offs


OFF = _layout_offsets()
VMEM_LIMIT = 48 * 1024 * 1024


def _alibi_slopes(n):
    return jnp.asarray([2.0 ** (-8.0 * (i + 1) / n) for i in range(n)], F32)


def _proj_kernel(x_ref, w_ref, o_ref, xb_ref):
    @pl.when(pl.program_id(1) == 0)
    def _():
        xb_ref[...] = x_ref[...].astype(BF16)

    o_ref[...] = jnp.dot(xb_ref[...], w_ref[...], preferred_element_type=F32)


def _project(x2d, w_bf16):
    m, k = x2d.shape
    n = w_bf16.shape[1]
    tm = next(t for t in (512, 256, 128, m) if m % t == 0)
    tn = 1024
    assert n % tn == 0
    return pl.pallas_call(
        _proj_kernel,
        grid=(m // tm, n // tn),
        in_specs=[pl.BlockSpec((tm, k), lambda i, j: (i, 0)),
                  pl.BlockSpec((k, tn), lambda i, j: (0, j))],
        out_specs=pl.BlockSpec((tm, tn), lambda i, j: (i, j)),
        out_shape=jax.ShapeDtypeStruct((m, n), F32),
        scratch_shapes=[pltpu.VMEM((tm, k), BF16)],
        compiler_params=pltpu.CompilerParams(dimension_semantics=("parallel", "arbitrary"),
                                             vmem_limit_bytes=VMEM_LIMIT),
        name="proj",
    )(x2d, w_bf16)


def _permute_w_in(w):
    cols = [w[:, SPLIT_OFFS[p]:SPLIT_OFFS[p] + SPLIT_SIZES[p]] for p in LAYOUT_ORDER]
    used = sum(SPLIT_SIZES)
    cols.append(jnp.zeros((w.shape[0], N_PROJ - used), w.dtype))
    return jnp.concatenate(cols, axis=1).astype(BF16)


def _piece(h, p):
    return h[..., OFF[p]:OFF[p] + SPLIT_SIZES[p]]


def _sigmoid(v):
    return 1.0 / (1.0 + jnp.exp(-v))


def _merge_kernel(oa_ref, ob_ref, oc_ref, za_ref, zb_ref, zc_ref, mg_ref, x_ref, wbr_ref, wout_ref, g_ref, b_ref, y_ref):
    m = None
    for n, (o_ref, z_ref) in enumerate(((oa_ref, za_ref), (ob_ref, zb_ref), (oc_ref, zc_ref))):
        z = z_ref[...]
        u = (o_ref[...] * (z * _sigmoid(z))).astype(BF16)
        br = jnp.dot(u, wbr_ref[n], preferred_element_type=F32)
        t = _sigmoid(mg_ref[:, n * D_MODEL:(n + 1) * D_MODEL]) * br
        m = t if m is None else m + t
    y = jnp.dot(m.astype(BF16), wout_ref[...], preferred_element_type=F32)
    r = ALPHA * x_ref[...] + y
    mu = jnp.mean(r, axis=-1, keepdims=True)
    rc = r - mu
    var = jnp.mean(rc * rc, axis=-1, keepdims=True)
    y_ref[...] = rc * lax.rsqrt(var + EPS) * g_ref[...] + b_ref[...]


def _merge(oa, ob, oc, h, x2d, wbr_bf16, wout_bf16, ln_g, ln_b):
    m = x2d.shape[0]
    tm = min(128, m)
    assert m % tm == 0
    w1 = W_A
    zspec = lambda p: pl.BlockSpec((tm, w1), lambda i, c=OFF[p] // w1: (i, c))
    ospec = pl.BlockSpec((tm, w1), lambda i: (i, 0))
    const = lambda shape: pl.BlockSpec(shape, lambda i: (0,) * len(shape), pipeline_mode=pl.Buffered(1))
    return pl.pallas_call(
        _merge_kernel,
        grid=(m // tm,),
        in_specs=[ospec, ospec, ospec, zspec(P_AZ), zspec(P_BZ), zspec(P_CZ),
                  pl.BlockSpec((tm, 3 * D_MODEL), lambda i, c=OFF[P_MG] // (3 * D_MODEL): (i, c)),
                  pl.BlockSpec((tm, D_MODEL), lambda i: (i, 0)),
                  const((3, w1, D_MODEL)), const((D_MODEL, D_MODEL)),
                  const((1, D_MODEL)), const((1, D_MODEL))],
        out_specs=pl.BlockSpec((tm, D_MODEL), lambda i: (i, 0)),
        out_shape=jax.ShapeDtypeStruct((m, D_MODEL), F32),
        compiler_params=pltpu.CompilerParams(dimension_semantics=("parallel",), vmem_limit_bytes=VMEM_LIMIT),
        name="merge",
    )(oa, ob, oc, h, h, h, h, x2d, wbr_bf16, wout_bf16, ln_g.reshape(1, -1), ln_b.reshape(1, -1))


def _dot_nt(a, b):
    return lax.dot_general(a, b, (((1,), (1,)), ((), ())), preferred_element_type=F32)


def _iota(shape, dim):
    return lax.broadcasted_iota(jnp.int32, shape, dim)


def _softmax_rows(s, mask):
    s = jnp.where(mask, s, -jnp.inf)
    m = jnp.max(s, axis=-1, keepdims=True)
    m = jnp.where(m == -jnp.inf, 0.0, m)
    e = jnp.where(mask, jnp.exp(s - m), 0.0)
    return e / jnp.maximum(jnp.sum(e, axis=-1, keepdims=True), 1e-30)


def _lam_init(li):
    return 0.8 - 0.6 * math.exp(-0.3 * li)


def _lam_value(lam_ref, li):
    lf = lam_ref[...]
    a = jnp.sum(lf[0:1] * lf[1:2], axis=-1, keepdims=True)
    b = jnp.sum(lf[2:3] * lf[3:4], axis=-1, keepdims=True)
    return jnp.exp(a) - jnp.exp(b) + _lam_init(li)


def _diff_norm(o, subg, li):
    return o * lax.rsqrt(jnp.mean(o * o, axis=-1, keepdims=True) + EPS) * subg * (1.0 - _lam_init(li))


def _diff_prompt_kernel(slopes_ref, lam_ref, subg_ref, q_ref, k_ref, v_ref, o_ref, *, tq, nq, li):
    hh = pl.program_id(1)
    i = pl.program_id(2)
    slope = slopes_ref[hh]
    lam_val = _lam_value(lam_ref, li)
    scale = D_C ** -0.5

    def body(kmax):
        q = q_ref[...]
        lane = _iota((1, 2 * D_C), 1)
        qlo = jnp.where(lane < D_C, q, 0.0).astype(BF16)
        qhi = jnp.where(lane >= D_C, q, 0.0).astype(BF16)
        k = k_ref[0:kmax, :].astype(BF16)
        v = v_ref[0:kmax, :].astype(BF16)
        dist = (i * tq + _iota((tq, kmax), 0)) - _iota((tq, kmax), 1)
        mask = dist >= 0
        bias = -slope * dist.astype(F32)
        p1 = _softmax_rows(_dot_nt(qlo, k) * scale + bias, mask)
        p2 = _softmax_rows(_dot_nt(qhi, k) * scale + bias, mask)
        a = p1 - lam_val * p2
        o = jnp.dot(a.astype(BF16), v, preferred_element_type=F32)
        o_ref[...] = _diff_norm(o, subg_ref[...], li)

    for j in range(nq):
        pl.when(i == j)(functools.partial(body, (j + 1) * tq))


def _diff_prompt(h2d, B, S, lam, sub_g, li):
    tq = 256 if S % 256 == 0 else 128
    nq = S // tq
    cb = lambda p: OFF[p] // LANE
    smem = pl.BlockSpec(memory_space=pltpu.SMEM)
    return pl.pallas_call(
        functools.partial(_diff_prompt_kernel, tq=tq, nq=nq, li=li),
        grid=(B, H_C, nq),
        in_specs=[smem,
                  pl.BlockSpec((4, D_C), lambda b, h, i: (0, 0)),
                  pl.BlockSpec((1, 2 * D_C), lambda b, h, i: (0, 0)),
                  pl.BlockSpec((tq, LANE), lambda b, h, i: (b * nq + i, cb(P_CQ) + h)),
                  pl.BlockSpec((S, LANE), lambda b, h, i: (b, cb(P_CK) + h)),
                  pl.BlockSpec((S, LANE), lambda b, h, i: (b, cb(P_CV) + h))],
        out_specs=pl.BlockSpec((tq, LANE), lambda b, h, i: (b * nq + i, h)),
        out_shape=jax.ShapeDtypeStruct((B * S, W_C), F32),
        compiler_params=pltpu.CompilerParams(dimension_semantics=("parallel", "parallel", "arbitrary"),
                                             vmem_limit_bytes=VMEM_LIMIT),
        name="diff_prompt",
    )(_alibi_slopes(H_C), lam, sub_g.reshape(1, -1), h2d, h2d, h2d)


def _head_slopes(hrow, n):
    out = jnp.zeros(hrow.shape, F32)
    for h in range(n):
        out = jnp.where(hrow == h, 2.0 ** (-8.0 * (h + 1) / n), out)
    return out


def _diffs_kernel(pt_ref, lam_ref, subg_ref, q_ref, new_ref, *rest, pps, n_steps, past, ds, li):
    page_refs = rest[:pps]
    o_ref = rest[pps]
    m_ref, l_ref, acc_ref = rest[pps + 1:]
    step = pl.program_id(1)
    nr = ds * H_C
    cols = PAGE_SIZE * H_C
    scale = D_C ** -0.5
    row = _iota((2 * nr, 1), 0)
    slope = _head_slopes(row % H_C, H_C)
    qpos = past + (row % nr) // H_C
    q = q_ref[0]
    lane = _iota((1, 2 * D_C), 1)
    qf = jnp.concatenate([jnp.where(lane < D_C, q, 0.0), jnp.where(lane >= D_C, q, 0.0)], axis=0)
    qb = qf.astype(BF16)

    @pl.when(step == 0)
    def _():
        m_ref[...] = jnp.full(m_ref.shape, -jnp.inf, F32)
        l_ref[...] = jnp.zeros(l_ref.shape, F32)
        acc_ref[...] = jnp.zeros(acc_ref.shape, F32)

    col = _iota((1, cols), 1)
    own = col % H_C == row % H_C
    for j, pref in enumerate(page_refs):
        k = pref[0, 0, :, 0].reshape(cols, 2 * D_C).astype(BF16)
        v = pref[0, 0, :, 1].reshape(cols, 2 * D_C).astype(BF16)
        dist = qpos - ((step * pps + j) * PAGE_SIZE + col // H_C)
        mask = own & (dist >= 0)
        s = jnp.where(mask, _dot_nt(qb, k) * scale - slope * dist.astype(F32), -jnp.inf)
        m_old = m_ref[...]
        m_new = jnp.maximum(m_old, jnp.max(s, axis=-1, keepdims=True))
        m_use = jnp.where(m_new == -jnp.inf, 0.0, m_new)
        alpha = jnp.exp(m_old - m_use)
        e = jnp.where(mask, jnp.exp(s - m_use), 0.0)
        l_ref[...] = alpha * l_ref[...] + jnp.sum(e, axis=-1, keepdims=True)
        acc_ref[...] = alpha * acc_ref[...] + jnp.dot(e.astype(BF16), v, preferred_element_type=F32)
        m_ref[...] = m_new

    @pl.when(step == n_steps - 1)
    def _():
        tile = lambda a: jnp.concatenate([a] * (2 * ds), axis=0)
        s_cols = []
        for t in range(ds):
            st = jnp.sum(qf * tile(new_ref[0, t, 0]), axis=-1, keepdims=True) * scale
            dist = qpos - (past + t)
            s_cols.append(jnp.where(dist >= 0, st - slope * dist.astype(F32), -jnp.inf))
        m_old = m_ref[...]
        m_new = m_old
        for st in s_cols:
            m_new = jnp.maximum(m_new, st)
        m_use = jnp.where(m_new == -jnp.inf, 0.0, m_new)
        alpha = jnp.exp(m_old - m_use)
        l = alpha * l_ref[...]
        acc = alpha * acc_ref[...]
        for t, st in enumerate(s_cols):
            e = jnp.where(st == -jnp.inf, 0.0, jnp.exp(st - m_use))
            l = l + e
            acc = acc + e * tile(new_ref[0, t, 1])
        o = acc / jnp.maximum(l, 1e-30)
        od = o[0:nr] - _lam_value(lam_ref, li) * o[nr:2 * nr]
        o_ref[0] = _diff_norm(od, subg_ref[...], li)


def _diffs(hs, cache_diff, page_table, lam, sub_g, li):
    DB, ds, _ = hs.shape
    n_pages = page_table.shape[1]
    pps = _pages_per_step(n_pages)
    n_steps = n_pages // pps
    nr = H_C * ds
    q3 = _piece(hs, P_CQ).reshape(DB, nr, 2 * D_C)
    new5 = hs[..., OFF[P_CK]:OFF[P_CV] + W_C].reshape(DB, ds, 2, H_C, 2 * D_C)
    page_spec = lambda j: pl.BlockSpec((1, 1, PAGE_SIZE, 2, H_C, 2 * D_C),
                                       lambda b, s, pt, j=j: (li, pt[b, s * pps + j], 0, 0, 0, 0))
    grid_spec = pltpu.PrefetchScalarGridSpec(
        num_scalar_prefetch=1,
        grid=(DB, n_steps),
        in_specs=[pl.BlockSpec((4, D_C), lambda b, s, pt: (0, 0)),
                  pl.BlockSpec((1, 2 * D_C), lambda b, s, pt: (0, 0)),
                  pl.BlockSpec((1, nr, 2 * D_C), lambda b, s, pt: (b, 0, 0)),
                  pl.BlockSpec((1, ds, 2, H_C, 2 * D_C), lambda b, s, pt: (b, 0, 0, 0, 0))]
        + [page_spec(j) for j in range(pps)],
        out_specs=pl.BlockSpec((1, nr, 2 * D_C), lambda b, s, pt: (b, 0, 0)),
        scratch_shapes=[pltpu.VMEM((2 * nr, 1), F32), pltpu.VMEM((2 * nr, 1), F32), pltpu.VMEM((2 * nr, 2 * D_C), F32)],
    )
    o = pl.pallas_call(
        functools.partial(_diffs_kernel, pps=pps, n_steps=n_steps, past=n_pages * PAGE_SIZE, ds=ds, li=li),
        grid_spec=grid_spec,
        out_shape=jax.ShapeDtypeStruct((DB, nr, 2 * D_C), F32),
        compiler_params=pltpu.CompilerParams(dimension_semantics=("parallel", "arbitrary"),
                                             vmem_limit_bytes=VMEM_LIMIT),
        name="diff_sample",
    )(page_table, lam, sub_g.reshape(1, -1), q3, new5, *([cache_diff] * pps))
    return o.reshape(DB, ds, W_C)


def _pages_per_step(n_pages):
    return next(t for t in (4, 3, 2, 1) if n_pages % t == 0)


INT_MIN = -2 ** 31
KEY_NEG_INF = (0xFF800000 ^ 0x7FFFFFFF) - 2 ** 32


def _sort_key(x):
    bits = pltpu.bitcast(x, jnp.int32)
    return jnp.where(bits < 0, bits ^ 0x7FFFFFFF, bits)


def _kth_largest_key(skey, k):
    def step(t, acc):
        cand = acc | lax.shift_left(jnp.int32(1), 31 - t)
        cnt = jnp.sum(jnp.where(skey >= (cand ^ INT_MIN), 1.0, 0.0), axis=-1, keepdims=True)
        return jnp.where(cnt >= k, cand, acc)

    acc = lax.fori_loop(0, 32, step, jnp.zeros((skey.shape[0], 1), jnp.int32))
    return acc ^ INT_MIN


def _topk_select(skey, k, sel_ref, width):
    thr = _kth_largest_key(skey, k)
    ge = skey >= thr
    sel_ref[:, 0:width] = jnp.where(ge, 1.0, 0.0)
    cnt_ge = jnp.sum(jnp.where(ge, 1.0, 0.0), axis=-1, keepdims=True)
    tied = jnp.where((cnt_ge > k) & (thr != KEY_NEG_INF), 1.0, 0.0)

    @pl.when(jnp.max(tied) > 0.0)
    def _():
        gt = skey > thr
        need = k - jnp.sum(jnp.where(gt, 1.0, 0.0), axis=-1, keepdims=True)
        ch = 256 if width % 256 == 0 else LANE
        tri = jnp.where(_iota((ch, ch), 0) <= _iota((ch, ch), 1), 1.0, 0.0).astype(BF16)
        run = jnp.zeros((skey.shape[0], 1), F32)
        for c in range(0, width, ch):
            eq = jnp.where(skey[:, c:c + ch] == thr, 1.0, 0.0)
            pre = jnp.dot(eq.astype(BF16), tri, preferred_element_type=F32) + run
            keep = gt[:, c:c + ch] | ((eq > 0.0) & (pre <= need))
            sel_ref[:, c:c + ch] = jnp.where(keep, 1.0, 0.0)
            run = run + jnp.sum(eq, axis=-1, keepdims=True)


def _dsa_prompt_kernel(slopes_ref, iq_ref, misc_ref, q_ref, k_ref, v_ref, ik_ref, o_ref, sel_ref, *, tq, nq, k_sel):
    i = pl.program_id(1)

    def body(kmax):
        ik = ik_ref[0:kmax, :].astype(BF16)
        score = jnp.zeros((tq, kmax), F32)
        for h in range(H_I):
            iqh = iq_ref[:, h * D_I:(h + 1) * D_I].astype(BF16)
            rel = jnp.maximum(_dot_nt(iqh, ik) * D_I ** -0.5, 0.0)
            w = misc_ref[:, 3 * H_A + h:3 * H_A + h + 1] * H_I ** -0.5
            score = score + w * rel
        dist = (i * tq + _iota((tq, kmax), 0)) - _iota((tq, kmax), 1)
        causal = dist >= 0
        skey = _sort_key(jnp.where(causal, score, -jnp.inf))
        _topk_select(skey, k_sel, sel_ref, kmax)
        mask = (sel_ref[:, 0:kmax] > 0.0) & causal
        distf = dist.astype(F32)
        k = k_ref[0:kmax, :].astype(BF16)
        v = v_ref[0:kmax, :].astype(BF16)
        for h in range(H_B):
            qh = q_ref[:, h * DH_B:(h + 1) * DH_B].astype(BF16)
            s = _dot_nt(qh, k) * DH_B ** -0.5 - slopes_ref[h] * distf
            p = _softmax_rows(s, mask)
            o_ref[:, h * DH_B:(h + 1) * DH_B] = jnp.dot(p.astype(BF16), v, preferred_element_type=F32)

    lo = 0
    for kmax in _key_ranges(nq * tq):
        hi = kmax // tq
        pl.when((i >= lo) & (i < hi))(functools.partial(body, kmax))
        lo = hi


def _key_ranges(S, n=2, align=256):
    return sorted({min(S, -(-(S * (v + 1) // n) // align) * align) for v in range(n)})


def _dsa_prompt(h2d, B, S):
    tq = 128
    nq = S // tq
    k_sel = min(TOPK_MAX, S // 4)
    cb = lambda p, w=LANE: OFF[p] // w
    kspec = lambda c: pl.BlockSpec((S, LANE), lambda b, i, c=c: (b, c))
    return pl.pallas_call(
        functools.partial(_dsa_prompt_kernel, tq=tq, nq=nq, k_sel=k_sel),
        grid=(B, nq),
        in_specs=[pl.BlockSpec(memory_space=pltpu.SMEM),
                  pl.BlockSpec((tq, H_I * D_I), lambda b, i: (b * nq + i, cb(P_BIQ, H_I * D_I))),
                  pl.BlockSpec((tq, LANE), lambda b, i: (b * nq + i, cb(P_AG))),
                  pl.BlockSpec((tq, W_B), lambda b, i: (b * nq + i, cb(P_BQ, W_B))),
                  kspec(cb(P_BKV)), kspec(cb(P_BKV) + 1), kspec(cb(P_BIK))],
        out_specs=pl.BlockSpec((tq, W_B), lambda b, i: (b * nq + i, 0)),
        out_shape=jax.ShapeDtypeStruct((B * S, W_B), F32),
        scratch_shapes=[pltpu.VMEM((tq, S), F32)],
        compiler_params=pltpu.CompilerParams(dimension_semantics=("parallel", "arbitrary"),
                                             vmem_limit_bytes=VMEM_LIMIT),
        name="dsa_prompt",
    )(_alibi_slopes(H_B), h2d, h2d, h2d, h2d, h2d, h2d)


def _compress_chunks(a, pos_ref, w_ref, c):
    n_ch = a.shape[0]
    t0 = jnp.dot((a + pos_ref[c, 0]).astype(BF16), w_ref[c, 0], preferred_element_type=F32)
    t1 = jnp.dot((a + pos_ref[c, 1]).astype(BF16), w_ref[c, 1], preferred_element_type=F32)
    return t0 + pltpu.roll(t1, n_ch - 1, 0)


def _cmp_prompt_kernel(x_ref, pos_ref, w_ref, o_ref, *, n_ch):
    a = jnp.concatenate([x_ref[pl.ds(l, n_ch, stride=CMP_STRIDE), :] for l in range(CMP_STRIDE)], axis=1)
    o_ref[0, 0] = _compress_chunks(a, pos_ref, w_ref, 0)


def _cmp_weights(w_cmp, cmp_pos):
    R = CMP_LEN // CMP_STRIDE
    return (w_cmp.reshape(2, R, CMP_STRIDE * DH_A, DH_A).astype(BF16),
            cmp_pos.reshape(2, R, 1, CMP_STRIDE * DH_A))


def _cmp_prompt(h2d, B, S, w_cmp, cmp_pos):
    assert S % (8 * CMP_STRIDE) == 0 and CMP_LEN == 2 * CMP_STRIDE
    n_ch = S // CMP_STRIDE
    w4, p4 = _cmp_weights(w_cmp, cmp_pos)
    kd = CMP_STRIDE * DH_A
    return pl.pallas_call(
        functools.partial(_cmp_prompt_kernel, n_ch=n_ch),
        grid=(B, 2 * KV_A),
        in_specs=[pl.BlockSpec((S, LANE), lambda b, pc: (b, OFF[P_AKV] // LANE + pc)),
                  pl.BlockSpec((1, 2, 1, kd), lambda b, pc: (pc // KV_A, 0, 0, 0)),
                  pl.BlockSpec((1, 2, kd, DH_A), lambda b, pc: (pc // KV_A, 0, 0, 0))],
        out_specs=pl.BlockSpec((1, 1, n_ch, DH_A), lambda b, pc: (b, pc, 0, 0)),
        out_shape=jax.ShapeDtypeStruct((B, 2 * KV_A, n_ch, DH_A), F32),
        compiler_params=pltpu.CompilerParams(dimension_semantics=("parallel", "arbitrary"),
                                             vmem_limit_bytes=VMEM_LIMIT),
        name="cmp_prompt",
    )(h2d, p4, w4)


def _split3(x):
    hi = x.astype(BF16)
    r1 = x - hi.astype(F32)
    mid = r1.astype(BF16)
    lo = (r1 - mid.astype(F32)).astype(BF16)
    return hi, mid, lo


def _block_select(psum, qpos, n_cmp, n_sel):
    rows, ncp = psum.shape
    jj = _iota((ncp, LANE), 0) * CMP_STRIDE
    ii = _iota((ncp, LANE), 1) * SEL_BLOCK
    ov = jnp.maximum(jnp.minimum(jj + CMP_LEN, ii + SEL_BLOCK) - jnp.maximum(jj, ii), 0)
    ovm = (ov.astype(F32) * (1.0 / CMP_LEN)).astype(BF16)
    imp = None
    for part in _split3(psum):
        t = jnp.dot(part, ovm, preferred_element_type=F32)
        imp = t if imp is None else imp + t
    ib = _iota((rows, LANE), 1)
    cur = qpos // SEL_BLOCK
    valid = (ib * SEL_BLOCK <= qpos) & (ib < n_sel)
    forced = (ib == 0) | (ib == cur) | (ib == cur - 1)
    imp = jnp.where(valid, imp + jnp.where(forced, FORCE_BONUS, 0.0), -jnp.inf)
    rank = jnp.zeros((rows, LANE), F32)
    for j in range(n_sel):
        col = imp[:, j:j + 1]
        ahead = (col > imp) | ((col == imp) & (ib > j))
        rank = rank + jnp.where(ahead, 1.0, 0.0)
    return jnp.where((rank < min(SEL_TOP, n_sel)) & (ib < n_sel), 1.0, 0.0)


def _cmpsel_prompt_kernel(slopes_ref, q_ref, kc_ref, ocmp_ref, sel_ref, *, tq, n_ch, n_sel):
    c = pl.program_id(1)
    i = pl.program_id(2)
    scale = DH_A ** -0.5
    n_cmp = n_ch - 1
    kck = kc_ref[0, c].astype(BF16)
    kcv = kc_ref[0, KV_A + c].astype(BF16)
    qpos = i * tq + _iota((tq, 1), 0)
    col = _iota((1, n_ch), 1)
    dist = qpos - (col * CMP_STRIDE + (CMP_LEN - 1))
    mask = (dist >= 0) & (col < n_cmp)
    distf = dist.astype(F32)
    psum = jnp.zeros((tq, n_ch), F32)
    for g in range(G_A):
        qg = q_ref[:, g * DH_A:(g + 1) * DH_A].astype(BF16)
        s = _dot_nt(qg, kck) * scale - slopes_ref[c * G_A + g] * distf
        p = _softmax_rows(s, mask)
        ocmp_ref[:, g * DH_A:(g + 1) * DH_A] = jnp.dot(p.astype(BF16), kcv, preferred_element_type=F32)
        psum = psum + p
    sel_ref[0, 0] = _block_select(psum, qpos, n_cmp, n_sel)


def _cmpsel_prompt(h2d, kc, B, S):
    tq = 256 if S % 256 == 0 else 128
    nq = S // tq
    n_ch = S // CMP_STRIDE
    n_sel = S // SEL_BLOCK
    gw = G_A * DH_A
    return pl.pallas_call(
        functools.partial(_cmpsel_prompt_kernel, tq=tq, n_ch=n_ch, n_sel=n_sel),
        grid=(B, KV_A, nq),
        in_specs=[pl.BlockSpec(memory_space=pltpu.SMEM),
                  pl.BlockSpec((tq, gw), lambda b, c, i: (b * nq + i, OFF[P_AQ] // gw + c)),
                  pl.BlockSpec((1, 2 * KV_A, n_ch, DH_A), lambda b, c, i: (b, 0, 0, 0))],
        out_specs=[pl.BlockSpec((tq, gw), lambda b, c, i: (b * nq + i, c)),
                   pl.BlockSpec((1, 1, tq, LANE), lambda b, c, i: (b, c, i, 0))],
        out_shape=[jax.ShapeDtypeStruct((B * S, W_A), F32), jax.ShapeDtypeStruct((B, KV_A, S, LANE), F32)],
        compiler_params=pltpu.CompilerParams(dimension_semantics=("parallel", "parallel", "arbitrary"),
                                             vmem_limit_bytes=VMEM_LIMIT),
        name="cmpsel_prompt",
    )(_alibi_slopes(H_A), h2d, kc)


def _gate(misc, lane_idx, col):
    return jnp.sum(jnp.where(lane_idx == col, _sigmoid(misc), 0.0), axis=-1, keepdims=True)


def _slcwin_prompt_kernel(slopes_ref, q_ref, misc_ref, ocmp_ref, sel_ref, sk_ref, sv_ref, wk_ref, wv_ref, o_ref, *, tq, nq):
    hh = pl.program_id(1)
    i = pl.program_id(2)
    scale = DH_A ** -0.5
    slope = slopes_ref[hh]

    def body(j):
        q0 = j * tq
        kmax = q0 + tq
        q = q_ref[...].astype(BF16)
        qpos = q0 + _iota((tq, 1), 0)
        ex = jnp.where(_iota((LANE, kmax), 1) // SEL_BLOCK == _iota((LANE, kmax), 0), 1.0, 0.0).astype(BF16)
        keym = jnp.dot(sel_ref[0, 0].astype(BF16), ex, preferred_element_type=F32)
        dist = qpos - _iota((1, kmax), 1)
        mask = (keym > 0.5) & (dist >= 0)
        s = _dot_nt(q, sk_ref[0:kmax, :].astype(BF16)) * scale - slope * dist.astype(F32)
        o_slc = jnp.dot(_softmax_rows(s, mask).astype(BF16), sv_ref[0:kmax, :].astype(BF16), preferred_element_type=F32)
        ws = max(0, q0 - WINDOW)
        dw = qpos - (ws + _iota((1, kmax - ws), 1))
        mw = (dw >= 0) & (dw <= WINDOW)
        sw = _dot_nt(q, wk_ref[ws:kmax, :].astype(BF16)) * scale - slope * dw.astype(F32)
        o_win = jnp.dot(_softmax_rows(sw, mw).astype(BF16), wv_ref[ws:kmax, :].astype(BF16), preferred_element_type=F32)
        misc = misc_ref[...]
        lane = _iota((tq, LANE), 1)
        o_ref[...] = (_gate(misc, lane, 3 * hh) * ocmp_ref[...] + _gate(misc, lane, 3 * hh + 1) * o_slc
                      + _gate(misc, lane, 3 * hh + 2) * o_win)

    for j in range(nq):
        pl.when(i == j)(functools.partial(body, j))


def _slcwin_prompt(h2d, ocmp, sel, B, S):
    tq = 256 if S % 256 == 0 else 128
    nq = S // tq
    akv = OFF[P_AKV] // LANE
    row = lambda w, cfn: pl.BlockSpec((tq, w), lambda b, h, i: (b * nq + i, cfn(h)))
    kvs = lambda piece: pl.BlockSpec((S, LANE), lambda b, h, i: (b, akv + piece * KV_A + h // G_A))
    return pl.pallas_call(
        functools.partial(_slcwin_prompt_kernel, tq=tq, nq=nq),
        grid=(B, H_A, nq),
        in_specs=[pl.BlockSpec(memory_space=pltpu.SMEM),
                  row(LANE, lambda h: OFF[P_AQ] // LANE + h),
                  row(LANE, lambda h: OFF[P_AG] // LANE),
                  row(LANE, lambda h: h),
                  pl.BlockSpec((1, 1, tq, LANE), lambda b, h, i: (b, h // G_A, i, 0)),
                  kvs(2), kvs(3), kvs(4), kvs(5)],
        out_specs=row(LANE, lambda h: h),
        out_shape=jax.ShapeDtypeStruct((B * S, W_A), F32),
        compiler_params=pltpu.CompilerParams(dimension_semantics=("parallel", "parallel", "arbitrary"),
                                             vmem_limit_bytes=VMEM_LIMIT),
        name="slcwin_prompt",
    )(_alibi_slopes(H_A), h2d, h2d, ocmp, sel, h2d, h2d, h2d, h2d)


def _nsa_prompt(h2d, B, S, w_cmp, cmp_pos):
    kc = _cmp_prompt(h2d, B, S, w_cmp, cmp_pos)
    ocmp, sel = _cmpsel_prompt(h2d, kc, B, S)
    return _slcwin_prompt(h2d, ocmp, sel, B, S)


def _dsa_sample_kernel(pt_ref, iq_ref, iw_ref, q_ref, new_ref, *rest, n_pages, ds, k_sel):
    page_refs = rest[:n_pages]
    o_ref = rest[n_pages]
    score_ref, sel_ref, s_ref = rest[n_pages + 1:]
    past = n_pages * PAGE_SIZE
    width = past + LANE
    lane = _iota((1, LANE), 1)
    new = new_ref[0]
    iqf = iq_ref[0]
    iq = iqf.astype(BF16)
    iw = iw_ref[0] * H_I ** -0.5
    score_ref[...] = jnp.full(score_ref.shape, -jnp.inf, F32)
    for p, pref in enumerate(page_refs):
        ik = pref[0, 0, 2].astype(BF16)
        rel = jnp.maximum(_dot_nt(iq, ik) * D_I ** -0.5, 0.0) * iw
        score_ref[0:ds, p * PAGE_SIZE:(p + 1) * PAGE_SIZE] = jnp.sum(rel.reshape(ds, H_I, PAGE_SIZE), axis=1)
    qi = _iota((ds, 1), 0)
    tile = jnp.full((ds, LANE), -jnp.inf, F32)
    for t in range(ds):
        rel = jnp.maximum(jnp.sum(iqf * new[t:t + 1, 2 * DH_B:3 * DH_B], axis=-1, keepdims=True) * D_I ** -0.5, 0.0) * iw
        sc = jnp.sum(rel.reshape(ds, H_I, 1), axis=1)
        tile = jnp.where((lane == t) & (qi >= t), sc, tile)
    score_ref[0:ds, past:width] = tile
    _topk_select(_sort_key(score_ref[...]), k_sel, sel_ref, width)
    nr = ds * H_B
    row = _iota((nr, 1), 0)
    slope = _head_slopes(row % H_B, H_B)
    qpos = past + row // H_B
    qf = q_ref[0]
    q = qf.astype(BF16)
    scale = DH_B ** -0.5

    def sel_rows(c0):
        s = sel_ref[0:ds, c0:c0 + LANE]
        return jnp.broadcast_to(s[:, None, :], (ds, H_B, LANE)).reshape(nr, LANE) > 0.0

    for p, pref in enumerate(page_refs):
        k = pref[0, 0, 0].astype(BF16)
        dist = qpos - (p * PAGE_SIZE + lane)
        s = _dot_nt(q, k) * scale - slope * dist.astype(F32)
        s_ref[:, p * PAGE_SIZE:(p + 1) * PAGE_SIZE] = jnp.where(sel_rows(p * PAGE_SIZE) & (dist >= 0), s, -jnp.inf)
    tile = jnp.full((nr, LANE), -jnp.inf, F32)
    for t in range(ds):
        dist = qpos - (past + t)
        st = jnp.sum(qf * new[t:t + 1, 0:DH_B], axis=-1, keepdims=True) * scale - slope * dist.astype(F32)
        tile = jnp.where((lane == t) & (dist >= 0), st, tile)
    s_ref[:, past:width] = jnp.where(sel_rows(past), tile, -jnp.inf)
    s = s_ref[...]
    m = jnp.max(s, axis=-1, keepdims=True)
    m = jnp.where(m == -jnp.inf, 0.0, m)
    e = jnp.exp(s - m)
    pr = e / jnp.maximum(jnp.sum(e, axis=-1, keepdims=True), 1e-30)
    o = jnp.zeros((nr, DH_B), F32)
    for p, pref in enumerate(page_refs):
        v = pref[0, 0, 1].astype(BF16)
        o = o + jnp.dot(pr[:, p * PAGE_SIZE:(p + 1) * PAGE_SIZE].astype(BF16), v, preferred_element_type=F32)
    for t in range(ds):
        o = o + pr[:, past + t:past + t + 1] * new[t:t + 1, DH_B:2 * DH_B]
    o_ref[0] = o


def _dsa_sample(hs, cache_dsa, page_table, li):
    DB, ds, _ = hs.shape
    n_pages = page_table.shape[1]
    past = n_pages * PAGE_SIZE
    width = past + LANE
    k_sel = min(TOPK_MAX, (past + ds) // 4)
    iq3 = _piece(hs, P_BIQ).reshape(DB, ds * H_I, D_I)
    iw3 = _piece(hs, P_BIW).reshape(DB, ds * H_I, 1)
    q3 = _piece(hs, P_BQ).reshape(DB, ds * H_B, DH_B)
    new3 = hs[..., OFF[P_BKV]:OFF[P_BIK] + D_I]
    per_b = lambda shape: pl.BlockSpec((1,) + shape, lambda b, pt: (b, 0, 0))
    page_spec = lambda p: pl.BlockSpec((1, 1, 3, PAGE_SIZE, DH_B), lambda b, pt, p=p: (li, pt[b, p], 0, 0, 0))
    grid_spec = pltpu.PrefetchScalarGridSpec(
        num_scalar_prefetch=1,
        grid=(DB,),
        in_specs=[per_b((ds * H_I, D_I)), per_b((ds * H_I, 1)), per_b((ds * H_B, DH_B)), per_b((ds, 3 * DH_B))]
        + [page_spec(p) for p in range(n_pages)],
        out_specs=per_b((ds * H_B, DH_B)),
        scratch_shapes=[pltpu.VMEM((8, width), F32), pltpu.VMEM((8, width), F32), pltpu.VMEM((ds * H_B, width), F32)],
    )
    o = pl.pallas_call(
        functools.partial(_dsa_sample_kernel, n_pages=n_pages, ds=ds, k_sel=k_sel),
        grid_spec=grid_spec,
        out_shape=jax.ShapeDtypeStruct((DB, ds * H_B, DH_B), F32),
        compiler_params=pltpu.CompilerParams(dimension_semantics=("parallel",), vmem_limit_bytes=VMEM_LIMIT),
        name="dsa_sample",
    )(page_table, iq3, iw3, q3, new3, *([cache_dsa] * n_pages))
    return o.reshape(DB, ds, W_B)


def _nsa_sample_kernel(pt_ref, q_ref, gate_ref, new_ref, wnew_ref, st_ref, pos_ref, w_ref, *rest, n_pages, ds, wbuf):
    page_refs = rest[:n_pages]
    o_ref = rest[n_pages]
    s_ref = rest[n_pages + 1]
    x_refs = rest[n_pages + 2:]
    past = n_pages * PAGE_SIZE
    width = past + LANE
    n_ch = past // CMP_STRIDE
    n_sel = -(-(past + ds) // SEL_BLOCK)
    nr = G_A * ds * KV_A
    scale = DH_A ** -0.5
    row = _iota((nr, 1), 0)
    grp = row % KV_A
    qpos = past + (row % (ds * KV_A)) // KV_A
    slope = _head_slopes(grp * G_A + row // (ds * KV_A), H_A)
    lane = _iota((1, LANE), 1)
    qf = q_ref[0]
    q = qf.astype(BF16)
    new = new_ref[0]

    def page_piece(pref, pc):
        return pref[0, 0, pl.ds(pc, PAGE_SIZE, stride=4 * KV_A), :]

    def state_piece(pc):
        return st_ref[0, 0, pl.ds(pc, wbuf, stride=2 * KV_A), :]

    def by_group(fn):
        out = fn(0)
        for c in range(1, KV_A):
            out = jnp.where(grp == c, fn(c), out)
        return out

    kc = []
    for pc, x_ref in enumerate(x_refs):
        for p, pref in enumerate(page_refs):
            x_ref[p * PAGE_SIZE:(p + 1) * PAGE_SIZE, :] = page_piece(pref, pc)
        a = jnp.concatenate([x_ref[pl.ds(l, n_ch, stride=CMP_STRIDE), :] for l in range(CMP_STRIDE)], axis=1)
        kc.append(_compress_chunks(a, pos_ref, w_ref, pc // KV_A).astype(BF16))
    col = _iota((1, n_ch), 1)
    dist_c = qpos - (col * CMP_STRIDE + (CMP_LEN - 1))
    mask_c = (dist_c >= 0) & (col < n_ch - 1)
    s = by_group(lambda c: _dot_nt(q, kc[c])) * scale - slope * dist_c.astype(F32)
    p_c = _softmax_rows(s, mask_c)
    p_cb = p_c.astype(BF16)
    o_cmp = by_group(lambda c: jnp.dot(p_cb, kc[KV_A + c], preferred_element_type=F32))
    rq = ds * KV_A
    psum = p_c[0:rq]
    for g in range(1, G_A):
        psum = psum + p_c[g * rq:(g + 1) * rq]
    sel8 = _block_select(psum, qpos[0:rq], n_ch - 1, n_sel)
    sel = jnp.concatenate([sel8] * G_A, axis=0)

    bpp = PAGE_SIZE // SEL_BLOCK
    for p, pref in enumerate(page_refs):
        keym = jnp.zeros((nr, PAGE_SIZE), F32)
        for t in range(bpp):
            keym = jnp.where(lane // SEL_BLOCK == t, sel[:, p * bpp + t:p * bpp + t + 1], keym)
        dist = qpos - (p * PAGE_SIZE + lane)
        sp = by_group(lambda c: _dot_nt(q, page_piece(pref, 2 * KV_A + c).astype(BF16)))
        sp = sp * scale - slope * dist.astype(F32)
        s_ref[:, p * PAGE_SIZE:(p + 1) * PAGE_SIZE] = jnp.where((keym > 0.5) & (dist >= 0), sp, -jnp.inf)
    tile = jnp.full((nr, LANE), -jnp.inf, F32)
    for t in range(ds):
        dist = qpos - (past + t)
        kt = by_group(lambda c: jnp.broadcast_to(new[t:t + 1, (2 * KV_A + c) * DH_A:(2 * KV_A + c + 1) * DH_A], (nr, DH_A)))
        st = jnp.sum(qf * kt, axis=-1, keepdims=True) * scale - slope * dist.astype(F32)
        tile = jnp.where((lane == t) & (dist >= 0), st, tile)
    s_ref[:, past:width] = jnp.where(sel[:, n_pages * bpp:n_pages * bpp + 1] > 0.5, tile, -jnp.inf)
    s = s_ref[...]
    m = jnp.max(s, axis=-1, keepdims=True)
    m = jnp.where(m == -jnp.inf, 0.0, m)
    e = jnp.exp(s - m)
    pr = e / jnp.maximum(jnp.sum(e, axis=-1, keepdims=True), 1e-30)
    o_slc = jnp.zeros((nr, DH_A), F32)
    for p, pref in enumerate(page_refs):
        pb = pr[:, p * PAGE_SIZE:(p + 1) * PAGE_SIZE].astype(BF16)
        o_slc = o_slc + by_group(lambda c: jnp.dot(
            pb, page_piece(pref, 3 * KV_A + c).astype(BF16), preferred_element_type=F32))
    for t in range(ds):
        vt = by_group(lambda c: jnp.broadcast_to(new[t:t + 1, (3 * KV_A + c) * DH_A:(3 * KV_A + c + 1) * DH_A], (nr, DH_A)))
        o_slc = o_slc + pr[:, past + t:past + t + 1] * vt

    wnew = wnew_ref[0]
    dw = qpos - (past - wbuf + _iota((1, wbuf), 1))
    sw = by_group(lambda c: _dot_nt(q, state_piece(c).astype(BF16)))
    sw = jnp.where((dw >= 0) & (dw <= WINDOW), sw * scale - slope * dw.astype(F32), -jnp.inf)
    tile = jnp.full((nr, LANE), -jnp.inf, F32)
    for t in range(ds):
        dist = qpos - (past + t)
        kt = by_group(lambda c: jnp.broadcast_to(wnew[t:t + 1, c * DH_A:(c + 1) * DH_A], (nr, DH_A)))
        st = jnp.sum(qf * kt, axis=-1, keepdims=True) * scale - slope * dist.astype(F32)
        tile = jnp.where((lane == t) & (dist >= 0) & (dist <= WINDOW), st, tile)
    m = jnp.maximum(jnp.max(sw, axis=-1, keepdims=True), jnp.max(tile, axis=-1, keepdims=True))
    m = jnp.where(m == -jnp.inf, 0.0, m)
    ew = jnp.exp(sw - m)
    et = jnp.exp(tile - m)
    den = jnp.maximum(jnp.sum(ew, axis=-1, keepdims=True) + jnp.sum(et, axis=-1, keepdims=True), 1e-30)
    pw = (ew / den).astype(BF16)
    o_win = by_group(lambda c: jnp.dot(pw, state_piece(KV_A + c).astype(BF16), preferred_element_type=F32))
    pt = et / den
    for t in range(ds):
        vt = by_group(lambda c: jnp.broadcast_to(wnew[t:t + 1, (KV_A + c) * DH_A:(KV_A + c + 1) * DH_A], (nr, DH_A)))
        o_win = o_win + pt[:, t:t + 1] * vt

    g = _sigmoid(gate_ref[0])
    o_ref[0] = g[:, 0:1] * o_cmp + g[:, 1:2] * o_slc + g[:, 2:3] * o_win


def _nsa_sample(hs, win_n, cache_nsa, state_win, page_table, w_cmp, cmp_pos, li):
    DB, ds, _ = hs.shape
    n_pages = page_table.shape[1]
    past = n_pages * PAGE_SIZE
    wbuf = state_win.shape[2] // (2 * KV_A)
    assert ds * KV_A == 8 and CMP_LEN == 2 * CMP_STRIDE
    nr = G_A * ds * KV_A
    regroup = lambda a: jnp.transpose(a.reshape(DB, ds, KV_A, G_A, -1), (0, 3, 1, 2, 4)).reshape(DB, nr, -1)
    q3 = regroup(_piece(hs, P_AQ))
    g3 = regroup(_piece(hs, P_AG))
    new3 = hs[..., OFF[P_AKV]:OFF[P_AKV] + 4 * KV_A * DH_A]
    wnew3 = win_n.reshape(DB, ds, 2 * KV_A * DH_A)
    w4, p4 = _cmp_weights(w_cmp, cmp_pos)
    per_b = lambda shape: pl.BlockSpec((1,) + shape, lambda b, pt: (b, 0, 0))
    full = lambda a: pl.BlockSpec(a.shape, lambda b, pt: (0,) * a.ndim)
    page_spec = lambda p: pl.BlockSpec((1, 1, PAGE_SIZE * 4 * KV_A, DH_A), lambda b, pt, p=p: (li, pt[b, p], 0, 0))
    grid_spec = pltpu.PrefetchScalarGridSpec(
        num_scalar_prefetch=1,
        grid=(DB,),
        in_specs=[per_b((nr, DH_A)), per_b((nr, 3)), per_b((ds, 4 * KV_A * DH_A)), per_b((ds, 2 * KV_A * DH_A)),
                  pl.BlockSpec((1, 1, wbuf * 2 * KV_A, DH_A), lambda b, pt: (li, b, 0, 0)), full(p4), full(w4)]
        + [page_spec(p) for p in range(n_pages)],
        out_specs=per_b((nr, DH_A)),
        scratch_shapes=[pltpu.VMEM((nr, past + LANE), F32)] + [pltpu.VMEM((past, DH_A), F32)] * (2 * KV_A),
    )
    o = pl.pallas_call(
        functools.partial(_nsa_sample_kernel, n_pages=n_pages, ds=ds, wbuf=wbuf),
        grid_spec=grid_spec,
        out_shape=jax.ShapeDtypeStruct((DB, nr, DH_A), F32),
        compiler_params=pltpu.CompilerParams(dimension_semantics=("parallel",), vmem_limit_bytes=VMEM_LIMIT),
        name="nsa_sample",
    )(page_table, q3, g3, new3, wnew3, state_win, p4, w4, *([cache_nsa] * n_pages))
    return jnp.transpose(o.reshape(DB, G_A, ds, KV_A, DH_A), (0, 2, 3, 1, 4)).reshape(DB, ds, W_A)


def _new_rows(h, B, S):
    a_kv = _piece(h, P_AKV).reshape(B, S, 6, KV_A, DH_A)
    dsa_rows = h[..., OFF[P_BKV]:OFF[P_BIK] + D_I].reshape(B, S, 3, DH_B)
    diff_rows = h[..., OFF[P_CK]:OFF[P_CV] + W_C].reshape(B, S, 2, H_C, 2 * D_C)
    return a_kv[:, :, :4], a_kv[:, :, 4:], dsa_rows, diff_rows


def _mix(x, h, o_a, o_b, o_c, wbr, wout, ln_g, ln_b):
    B, S, _ = x.shape
    m = B * S
    y = _merge(o_a.reshape(m, W_A), o_b.reshape(m, W_B), o_c.reshape(m, W_C), h.reshape(m, N_PROJ),
               x.reshape(m, D_MODEL), wbr, wout, ln_g, ln_b)
    return y.reshape(B, S, D_MODEL)


def kernel(x_prompt, x_sample, cache_nsa, cache_dsa, cache_diff, state_nsa_win, page_table,
           w_in, w_cmp, cmp_pos, lam, sub_g, w_branch, w_out, ln_g, ln_b):
    B, S, _ = x_prompt.shape
    DB, DS, _ = x_sample.shape
    wbuf = state_nsa_win.shape[2]
    assert wbuf <= page_table.shape[1] * PAGE_SIZE and DS <= CMP_STRIDE
    rows = lambda a: a.reshape(a.shape[:2] + (-1, a.shape[-1]))
    nsa_pages, win_state = rows(cache_nsa), rows(state_nsa_win)
    dsa_pages = jnp.transpose(cache_dsa, (0, 1, 3, 2, 4))
    xp, xs = x_prompt, x_sample
    outs = [[] for _ in range(8)]
    for li in range(DEPTH):
        w_l = _permute_w_in(w_in[li])
        wbr = w_branch[li].astype(BF16)
        wout = w_out[li].astype(BF16)
        hp = _project(xp.reshape(B * S, D_MODEL), w_l)
        oa_p = _nsa_prompt(hp, B, S, w_cmp[li], cmp_pos[li])
        ob_p = _dsa_prompt(hp, B, S)
        oc_p = _diff_prompt(hp, B, S, lam[li], sub_g[li], li)
        hp = hp.reshape(B, S, N_PROJ)
        nsa_r, win_r, dsa_r, diff_r = _new_rows(hp, B, S)
        xp = _mix(xp, hp, oa_p, ob_p, oc_p, wbr, wout, ln_g[li], ln_b[li])
        hs = _project(xs.reshape(DB * DS, D_MODEL), w_l).reshape(DB, DS, N_PROJ)
        nsa_n, win_n, dsa_n, diff_n = _new_rows(hs, DB, DS)
        oa_s = _nsa_sample(hs, win_n, nsa_pages, win_state, page_table, w_cmp[li], cmp_pos[li], li)
        ob_s = _dsa_sample(hs, dsa_pages, page_table, li)
        oc_s = _diffs(hs, cache_diff, page_table, lam[li], sub_g[li], li)
        win_hist = jnp.concatenate([state_nsa_win[li], win_n], axis=1)
        xs = _mix(xs, hs, oa_s, ob_s, oc_s, wbr, wout, ln_g[li], ln_b[li])
        for lst, v in zip(outs, (nsa_r, nsa_n, dsa_r, dsa_n, diff_r, diff_n,
                                 win_r[:, S - min(WINDOW, S):], win_hist[:, win_hist.shape[1] - wbuf:])):
            lst.append(v)
    return (xp, xs) + tuple(jnp.stack(l) for l in outs)
```

```python
import functools
import math

import jax
import jax.numpy as jnp
import numpy as np
from jax import lax
from jax.experimental import pallas as pl
from jax.experimental.pallas import tpu as pltpu

F32 = jnp.float32
BF16 = jnp.bfloat16

D_MODEL = 2048
DEPTH = 2
PAGE_SIZE = 128
H_A, KV_A, DH_A = 8, 2, 128
G_A = H_A // KV_A
W_A = H_A * DH_A
CMP_LEN, CMP_STRIDE = 32, 16
SEL_BLOCK, SEL_TOP = 64, 16
WINDOW = 512
FORCE_BONUS = 1.0e4
H_B, DH_B = 8, 128
W_B = H_B * DH_B
H_I, D_I = 16, 128
TOPK_MAX = 256
H_C, D_C = 8, 64
W_C = H_C * 2 * D_C
ALPHA = (2 * DEPTH) ** 0.25
EPS = 1e-5
SPLIT_SIZES = (W_A, 6 * KV_A * DH_A, 3 * H_A, W_A, W_B, 2 * DH_B, H_I * D_I, D_I, H_I, W_B, W_C, W_C, W_C, W_C, 3 * D_MODEL)
SPLIT_OFFS = tuple(int(v) for v in np.cumsum((0,) + SPLIT_SIZES[:-1]))

(P_AQ, P_AKV, P_AG, P_AZ, P_BQ, P_BKV, P_BIQ, P_BIK, P_BIW, P_BZ, P_CQ, P_CK, P_CV, P_CZ, P_MG) = range(15)
LAYOUT_ORDER = (P_AQ, P_AZ, P_BQ, P_BZ, P_CQ, P_CZ, P_MG, P_BIQ, P_CK, P_CV, P_AKV, P_BKV, P_BIK, P_AG, P_BIW)
LANE = 128
N_PROJ = 18432


def _layout_offsets():
    offs, o = {}, 0
    for p in LAYOUT_ORDER:
        offs[p] = o
        o += SPLIT_SIZES[p]
    assert o <= N_PROJ and N_PROJ % LANE == 0
    return offs


OFF = _layout_offsets()
VMEM_LIMIT = 48 * 1024 * 1024


def _alibi_slopes(n):
    return jnp.asarray([2.0 ** (-8.0 * (i + 1) / n) for i in range(n)], F32)


def _proj_kernel(x_ref, w_ref, o_ref, xb_ref):
    @pl.when(pl.program_id(1) == 0)
    def _():
        xb_ref[...] = x_ref[...].astype(BF16)

    o_ref[...] = jnp.dot(xb_ref[...], w_ref[...], preferred_element_type=F32)


def _project(x2d, w_bf16):
    m, k = x2d.shape
    n = w_bf16.shape[1]
    tm = next(t for t in (512, 256, 128, m) if m % t == 0)
    tn = 1024
    assert n % tn == 0
    return pl.pallas_call(
        _proj_kernel,
        grid=(m // tm, n // tn),
        in_specs=[pl.BlockSpec((tm, k), lambda i, j: (i, 0)),
                  pl.BlockSpec((k, tn), lambda i, j: (0, j))],
        out_specs=pl.BlockSpec((tm, tn), lambda i, j: (i, j)),
        out_shape=jax.ShapeDtypeStruct((m, n), F32),
        scratch_shapes=[pltpu.VMEM((tm, k), BF16)],
        compiler_params=pltpu.CompilerParams(dimension_semantics=("parallel", "arbitrary"),
                                             vmem_limit_bytes=VMEM_LIMIT),
        name="proj",
    )(x2d, w_bf16)


def _permute_w_in(w):
    cols = [w[:, SPLIT_OFFS[p]:SPLIT_OFFS[p] + SPLIT_SIZES[p]] for p in LAYOUT_ORDER]
    used = sum(SPLIT_SIZES)
    cols.append(jnp.zeros((w.shape[0], N_PROJ - used), w.dtype))
    return jnp.concatenate(cols, axis=1).astype(BF16)


def _piece(h, p):
    return h[..., OFF[p]:OFF[p] + SPLIT_SIZES[p]]


def _sigmoid(v):
    return 1.0 / (1.0 + jnp.exp(-v))


def _merge_kernel(oa_ref, ob_ref, oc_ref, za_ref, zb_ref, zc_ref, mg_ref, x_ref, wbr_ref, wout_ref, g_ref, b_ref, y_ref):
    m = None
    for n, (o_ref, z_ref) in enumerate(((oa_ref, za_ref), (ob_ref, zb_ref), (oc_ref, zc_ref))):
        z = z_ref[...]
        u = (o_ref[...] * (z * _sigmoid(z))).astype(BF16)
        br = jnp.dot(u, wbr_ref[n], preferred_element_type=F32)
        t = _sigmoid(mg_ref[:, n * D_MODEL:(n + 1) * D_MODEL]) * br
        m = t if m is None else m + t
    y = jnp.dot(m.astype(BF16), wout_ref[...], preferred_element_type=F32)
    r = ALPHA * x_ref[...] + y
    mu = jnp.mean(r, axis=-1, keepdims=True)
    rc = r - mu
    var = jnp.mean(rc * rc, axis=-1, keepdims=True)
    y_ref[...] = rc * lax.rsqrt(var + EPS) * g_ref[...] + b_ref[...]


def _merge(oa, ob, oc, h, x2d, wbr_bf16, wout_bf16, ln_g, ln_b):
    m = x2d.shape[0]
    tm = min(128, m)
    assert m % tm == 0
    w1 = W_A
    zspec = lambda p: pl.BlockSpec((tm, w1), lambda i, c=OFF[p] // w1: (i, c))
    ospec = pl.BlockSpec((tm, w1), lambda i: (i, 0))
    const = lambda shape: pl.BlockSpec(shape, lambda i: (0,) * len(shape), pipeline_mode=pl.Buffered(1))
    return pl.pallas_call(
        _merge_kernel,
        grid=(m // tm,),
        in_specs=[ospec, ospec, ospec, zspec(P_AZ), zspec(P_BZ), zspec(P_CZ),
                  pl.BlockSpec((tm, 3 * D_MODEL), lambda i, c=OFF[P_MG] // (3 * D_MODEL): (i, c)),
                  pl.BlockSpec((tm, D_MODEL), lambda i: (i, 0)),
                  const((3, w1, D_MODEL)), const((D_MODEL, D_MODEL)),
                  const((1, D_MODEL)), const((1, D_MODEL))],
        out_specs=pl.BlockSpec((tm, D_MODEL), lambda i: (i, 0)),
        out_shape=jax.ShapeDtypeStruct((m, D_MODEL), F32),
        compiler_params=pltpu.CompilerParams(dimension_semantics=("parallel",), vmem_limit_bytes=VMEM_LIMIT),
        name="merge",
    )(oa, ob, oc, h, h, h, h, x2d, wbr_bf16, wout_bf16, ln_g.reshape(1, -1), ln_b.reshape(1, -1))


def _dot_nt(a, b):
    return lax.dot_general(a, b, (((1,), (1,)), ((), ())), preferred_element_type=F32)


def _iota(shape, dim):
    return lax.broadcasted_iota(jnp.int32, shape, dim)


def _softmax_rows(s, mask):
    s = jnp.where(mask, s, -jnp.inf)
    m = jnp.max(s, axis=-1, keepdims=True)
    m = jnp.where(m == -jnp.inf, 0.0, m)
    e = jnp.where(mask, jnp.exp(s - m), 0.0)
    return e / jnp.maximum(jnp.sum(e, axis=-1, keepdims=True), 1e-30)


def _lam_init(li):
    return 0.8 - 0.6 * math.exp(-0.3 * li)


def _lam_value(lam_ref, li):
    lf = lam_ref[...]
    a = jnp.sum(lf[0:1] * lf[1:2], axis=-1, keepdims=True)
    b = jnp.sum(lf[2:3] * lf[3:4], axis=-1, keepdims=True)
    return jnp.exp(a) - jnp.exp(b) + _lam_init(li)


def _diff_norm(o, subg, li):
    return o * lax.rsqrt(jnp.mean(o * o, axis=-1, keepdims=True) + EPS) * subg * (1.0 - _lam_init(li))


def _diff_prompt_kernel(slopes_ref, lam_ref, subg_ref, q_ref, k_ref, v_ref, o_ref, *, tq, nq, li):
    hh = pl.program_id(1)
    i = pl.program_id(2)
    slope = slopes_ref[hh]
    lam_val = _lam_value(lam_ref, li)
    scale = D_C ** -0.5

    def body(kmax):
        q = q_ref[...]
        lane = _iota((1, 2 * D_C), 1)
        qlo = jnp.where(lane < D_C, q, 0.0).astype(BF16)
        qhi = jnp.where(lane >= D_C, q, 0.0).astype(BF16)
        k = k_ref[0:kmax, :].astype(BF16)
        v = v_ref[0:kmax, :].astype(BF16)
        dist = (i * tq + _iota((tq, kmax), 0)) - _iota((tq, kmax), 1)
        mask = dist >= 0
        bias = -slope * dist.astype(F32)
        p1 = _softmax_rows(_dot_nt(qlo, k) * scale + bias, mask)
        p2 = _softmax_rows(_dot_nt(qhi, k) * scale + bias, mask)
        a = p1 - lam_val * p2
        o = jnp.dot(a.astype(BF16), v, preferred_element_type=F32)
        o_ref[...] = _diff_norm(o, subg_ref[...], li)

    for j in range(nq):
        pl.when(i == j)(functools.partial(body, (j + 1) * tq))


def _diff_prompt(h2d, B, S, lam, sub_g, li):
    tq = 256 if S % 256 == 0 else 128
    nq = S // tq
    cb = lambda p: OFF[p] // LANE
    smem = pl.BlockSpec(memory_space=pltpu.SMEM)
    return pl.pallas_call(
        functools.partial(_diff_prompt_kernel, tq=tq, nq=nq, li=li),
        grid=(B, H_C, nq),
        in_specs=[smem,
                  pl.BlockSpec((4, D_C), lambda b, h, i: (0, 0)),
                  pl.BlockSpec((1, 2 * D_C), lambda b, h, i: (0, 0)),
                  pl.BlockSpec((tq, LANE), lambda b, h, i: (b * nq + i, cb(P_CQ) + h)),
                  pl.BlockSpec((S, LANE), lambda b, h, i: (b, cb(P_CK) + h)),
                  pl.BlockSpec((S, LANE), lambda b, h, i: (b, cb(P_CV) + h))],
        out_specs=pl.BlockSpec((tq, LANE), lambda b, h, i: (b * nq + i, h)),
        out_shape=jax.ShapeDtypeStruct((B * S, W_C), F32),
        compiler_params=pltpu.CompilerParams(dimension_semantics=("parallel", "parallel", "arbitrary"),
                                             vmem_limit_bytes=VMEM_LIMIT),
        name="diff_prompt",
    )(_alibi_slopes(H_C), lam, sub_g.reshape(1, -1), h2d, h2d, h2d)


def _head_slopes(hrow, n):
    out = jnp.zeros(hrow.shape, F32)
    for h in range(n):
        out = jnp.where(hrow == h, 2.0 ** (-8.0 * (h + 1) / n), out)
    return out


def _diffs_kernel(pt_ref, lam_ref, subg_ref, q_ref, new_ref, *rest, pps, n_steps, past, ds, li):
    page_refs = rest[:pps]
    o_ref = rest[pps]
    m_ref, l_ref, acc_ref = rest[pps + 1:]
    step = pl.program_id(1)
    nr = ds * H_C
    cols = PAGE_SIZE * H_C
    scale = D_C ** -0.5
    row = _iota((2 * nr, 1), 0)
    slope = _head_slopes(row % H_C, H_C)
    qpos = past + (row % nr) // H_C
    q = q_ref[0]
    lane = _iota((1, 2 * D_C), 1)
    qf = jnp.concatenate([jnp.where(lane < D_C, q, 0.0), jnp.where(lane >= D_C, q, 0.0)], axis=0)
    qb = qf.astype(BF16)

    @pl.when(step == 0)
    def _():
        m_ref[...] = jnp.full(m_ref.shape, -jnp.inf, F32)
        l_ref[...] = jnp.zeros(l_ref.shape, F32)
        acc_ref[...] = jnp.zeros(acc_ref.shape, F32)

    col = _iota((1, cols), 1)
    bias0 = jnp.where(col % H_C == row % H_C, -slope * (qpos - col // H_C).astype(F32), -jnp.inf)
    ss = []
    for j, pref in enumerate(page_refs):
        k = pref[0, 0, :, 0].reshape(cols, 2 * D_C).astype(BF16)
        base = ((step * pps + j) * PAGE_SIZE).astype(F32)
        ss.append(_dot_nt(qb, k) * scale + (bias0 + slope * base))
    m_old = m_ref[...]
    m_new = m_old
    for s in ss:
        m_new = jnp.maximum(m_new, jnp.max(s, axis=-1, keepdims=True))
    m_use = jnp.where(m_new == -jnp.inf, 0.0, m_new)
    alpha = jnp.exp(m_old - m_use)
    l = alpha * l_ref[...]
    acc = alpha * acc_ref[...]
    for s, pref in zip(ss, page_refs):
        e = jnp.exp(s - m_use)
        l = l + jnp.sum(e, axis=-1, keepdims=True)
        v = pref[0, 0, :, 1].reshape(cols, 2 * D_C).astype(BF16)
        acc = acc + jnp.dot(e.astype(BF16), v, preferred_element_type=F32)
    l_ref[...] = l
    acc_ref[...] = acc
    m_ref[...] = m_new

    @pl.when(step == n_steps - 1)
    def _():
        tile = lambda a: jnp.concatenate([a] * (2 * ds), axis=0)
        s_cols = []
        for t in range(ds):
            st = jnp.sum(qf * tile(new_ref[0, t, 0]), axis=-1, keepdims=True) * scale
            dist = qpos - (past + t)
            s_cols.append(jnp.where(dist >= 0, st - slope * dist.astype(F32), -jnp.inf))
        m_old = m_ref[...]
        m_new = m_old
        for st in s_cols:
            m_new = jnp.maximum(m_new, st)
        m_use = jnp.where(m_new == -jnp.inf, 0.0, m_new)
        alpha = jnp.exp(m_old - m_use)
        l = alpha * l_ref[...]
        acc = alpha * acc_ref[...]
        for t, st in enumerate(s_cols):
            e = jnp.where(st == -jnp.inf, 0.0, jnp.exp(st - m_use))
            l = l + e
            acc = acc + e * tile(new_ref[0, t, 1])
        o = acc / jnp.maximum(l, 1e-30)
        od = o[0:nr] - _lam_value(lam_ref, li) * o[nr:2 * nr]
        o_ref[0] = _diff_norm(od, subg_ref[...], li)


def _diffs(hs, cache_diff, page_table, lam, sub_g, li):
    DB, ds, _ = hs.shape
    n_pages = page_table.shape[1]
    pps = _pages_per_step(n_pages)
    n_steps = n_pages // pps
    nr = H_C * ds
    q3 = _piece(hs, P_CQ).reshape(DB, nr, 2 * D_C)
    new5 = hs[..., OFF[P_CK]:OFF[P_CV] + W_C].reshape(DB, ds, 2, H_C, 2 * D_C)
    page_spec = lambda j: pl.BlockSpec((1, 1, PAGE_SIZE, 2, H_C, 2 * D_C),
                                       lambda b, s, pt, j=j: (li, pt[b, s * pps + j], 0, 0, 0, 0))
    grid_spec = pltpu.PrefetchScalarGridSpec(
        num_scalar_prefetch=1,
        grid=(DB, n_steps),
        in_specs=[pl.BlockSpec((4, D_C), lambda b, s, pt: (0, 0)),
                  pl.BlockSpec((1, 2 * D_C), lambda b, s, pt: (0, 0)),
                  pl.BlockSpec((1, nr, 2 * D_C), lambda b, s, pt: (b, 0, 0)),
                  pl.BlockSpec((1, ds, 2, H_C, 2 * D_C), lambda b, s, pt: (b, 0, 0, 0, 0))]
        + [page_spec(j) for j in range(pps)],
        out_specs=pl.BlockSpec((1, nr, 2 * D_C), lambda b, s, pt: (b, 0, 0)),
        scratch_shapes=[pltpu.VMEM((2 * nr, 1), F32), pltpu.VMEM((2 * nr, 1), F32), pltpu.VMEM((2 * nr, 2 * D_C), F32)],
    )
    o = pl.pallas_call(
        functools.partial(_diffs_kernel, pps=pps, n_steps=n_steps, past=n_pages * PAGE_SIZE, ds=ds, li=li),
        grid_spec=grid_spec,
        out_shape=jax.ShapeDtypeStruct((DB, nr, 2 * D_C), F32),
        compiler_params=pltpu.CompilerParams(dimension_semantics=("parallel", "arbitrary"),
                                             vmem_limit_bytes=VMEM_LIMIT),
        name="diff_sample",
    )(page_table, lam, sub_g.reshape(1, -1), q3, new5, *([cache_diff] * pps))
    return o.reshape(DB, ds, W_C)


def _pages_per_step(n_pages):
    return next(t for t in (8, 4, 3, 2, 1) if n_pages % t == 0)


INT_MIN = -2 ** 31
KEY_NEG_INF = (0xFF800000 ^ 0x7FFFFFFF) - 2 ** 32


def _sort_key(x):
    bits = pltpu.bitcast(x, jnp.int32)
    return jnp.where(bits < 0, bits ^ 0x7FFFFFFF, bits)


def _kth_largest_key(skey, k, two_bits):
    count = lambda cand: jnp.sum(jnp.where(skey >= (cand ^ INT_MIN), 1.0, 0.0), axis=-1, keepdims=True)

    def step(t, acc):
        cand = acc | lax.shift_left(jnp.int32(1), 31 - t)
        return jnp.where(count(cand) >= k, cand, acc)

    def step2(t, acc):
        c1 = acc | lax.shift_left(jnp.int32(1), 31 - 2 * t)
        lo = lax.shift_left(jnp.int32(1), 30 - 2 * t)
        c2 = acc | lo
        c3 = c1 | lo
        return jnp.where(count(c1) >= k, jnp.where(count(c3) >= k, c3, c1), jnp.where(count(c2) >= k, c2, acc))

    zero = jnp.zeros((skey.shape[0], 1), jnp.int32)
    acc = lax.fori_loop(0, 16, step2, zero) if two_bits else lax.fori_loop(0, 32, step, zero)
    return acc ^ INT_MIN


def _topk_select(skey, k, sel_ref, width, two_bits=False):
    thr = _kth_largest_key(skey, k, two_bits)
    ge = skey >= thr
    sel_ref[:, 0:width] = jnp.where(ge, 1.0, 0.0)
    cnt_ge = jnp.sum(jnp.where(ge, 1.0, 0.0), axis=-1, keepdims=True)
    tied = jnp.where((cnt_ge > k) & (thr != KEY_NEG_INF), 1.0, 0.0)

    @pl.when(jnp.max(tied) > 0.0)
    def _():
        gt = skey > thr
        need = k - jnp.sum(jnp.where(gt, 1.0, 0.0), axis=-1, keepdims=True)
        ch = 256 if width % 256 == 0 else LANE
        tri = jnp.where(_iota((ch, ch), 0) <= _iota((ch, ch), 1), 1.0, 0.0).astype(BF16)
        run = jnp.zeros((skey.shape[0], 1), F32)
        for c in range(0, width, ch):
            eq = jnp.where(skey[:, c:c + ch] == thr, 1.0, 0.0)
            pre = jnp.dot(eq.astype(BF16), tri, preferred_element_type=F32) + run
            keep = gt[:, c:c + ch] | ((eq > 0.0) & (pre <= need))
            sel_ref[:, c:c + ch] = jnp.where(keep, 1.0, 0.0)
            run = run + jnp.sum(eq, axis=-1, keepdims=True)


def _dsa_prompt_kernel(slopes_ref, iq_ref, misc_ref, q_ref, k_ref, v_ref, ik_ref, o_ref, sel_ref, *, tq, nq, k_sel):
    i = pl.program_id(1)

    def body(kmax):
        ik = ik_ref[0:kmax, :].astype(BF16)
        score = jnp.zeros((tq, kmax), F32)
        for h in range(H_I):
            iqh = iq_ref[:, h * D_I:(h + 1) * D_I].astype(BF16)
            w = misc_ref[:, 3 * H_A + h:3 * H_A + h + 1] * (H_I ** -0.5 * D_I ** -0.5)
            score = score + w * jnp.maximum(_dot_nt(iqh, ik), 0.0)
        dist = (i * tq + _iota((tq, kmax), 0)) - _iota((tq, kmax), 1)
        causal = dist >= 0
        skey = _sort_key(jnp.where(causal, score, -jnp.inf))
        _topk_select(skey, k_sel, sel_ref, kmax)
        mask = (sel_ref[:, 0:kmax] > 0.0) & causal
        distf = dist.astype(F32)
        k = k_ref[0:kmax, :].astype(BF16)
        v = v_ref[0:kmax, :].astype(BF16)
        for h in range(H_B):
            qh = q_ref[:, h * DH_B:(h + 1) * DH_B].astype(BF16)
            s = _dot_nt(qh, k) * DH_B ** -0.5 - slopes_ref[h] * distf
            p = _softmax_rows(s, mask)
            o_ref[:, h * DH_B:(h + 1) * DH_B] = jnp.dot(p.astype(BF16), v, preferred_element_type=F32)

    lo = 0
    for kmax in _key_ranges(nq * tq):
        hi = kmax // tq
        pl.when((i >= lo) & (i < hi))(functools.partial(body, kmax))
        lo = hi


def _key_ranges(S, n=2, align=256):
    return sorted({min(S, -(-(S * (v + 1) // n) // align) * align) for v in range(n)})


def _dsa_prompt(h2d, B, S):
    tq = 128
    nq = S // tq
    k_sel = min(TOPK_MAX, S // 4)
    cb = lambda p, w=LANE: OFF[p] // w
    kspec = lambda c: pl.BlockSpec((S, LANE), lambda b, i, c=c: (b, c))
    return pl.pallas_call(
        functools.partial(_dsa_prompt_kernel, tq=tq, nq=nq, k_sel=k_sel),
        grid=(B, nq),
        in_specs=[pl.BlockSpec(memory_space=pltpu.SMEM),
                  pl.BlockSpec((tq, H_I * D_I), lambda b, i: (b * nq + i, cb(P_BIQ, H_I * D_I))),
                  pl.BlockSpec((tq, LANE), lambda b, i: (b * nq + i, cb(P_AG))),
                  pl.BlockSpec((tq, W_B), lambda b, i: (b * nq + i, cb(P_BQ, W_B))),
                  kspec(cb(P_BKV)), kspec(cb(P_BKV) + 1), kspec(cb(P_BIK))],
        out_specs=pl.BlockSpec((tq, W_B), lambda b, i: (b * nq + i, 0)),
        out_shape=jax.ShapeDtypeStruct((B * S, W_B), F32),
        scratch_shapes=[pltpu.VMEM((tq, S), F32)],
        compiler_params=pltpu.CompilerParams(dimension_semantics=("parallel", "arbitrary"),
                                             vmem_limit_bytes=VMEM_LIMIT),
        name="dsa_prompt",
    )(_alibi_slopes(H_B), h2d, h2d, h2d, h2d, h2d, h2d)


def _compress_chunks(a, pos_ref, w_ref, c):
    n_ch = a.shape[0]
    t0 = jnp.dot((a + pos_ref[c, 0]).astype(BF16), w_ref[c, 0], preferred_element_type=F32)
    t1 = jnp.dot((a + pos_ref[c, 1]).astype(BF16), w_ref[c, 1], preferred_element_type=F32)
    return t0 + pltpu.roll(t1, n_ch - 1, 0)


def _cmp_prompt_kernel(x_ref, pos_ref, w_ref, o_ref, *, n_ch):
    a = jnp.concatenate([x_ref[pl.ds(l, n_ch, stride=CMP_STRIDE), :] for l in range(CMP_STRIDE)], axis=1)
    o_ref[0, 0] = _compress_chunks(a, pos_ref, w_ref, 0)


def _cmp_weights(w_cmp, cmp_pos):
    R = CMP_LEN // CMP_STRIDE
    return (w_cmp.reshape(2, R, CMP_STRIDE * DH_A, DH_A).astype(BF16),
            cmp_pos.reshape(2, R, 1, CMP_STRIDE * DH_A))


def _cmp_prompt(h2d, B, S, w_cmp, cmp_pos):
    assert S % (8 * CMP_STRIDE) == 0 and CMP_LEN == 2 * CMP_STRIDE
    n_ch = S // CMP_STRIDE
    w4, p4 = _cmp_weights(w_cmp, cmp_pos)
    kd = CMP_STRIDE * DH_A
    return pl.pallas_call(
        functools.partial(_cmp_prompt_kernel, n_ch=n_ch),
        grid=(B, 2 * KV_A),
        in_specs=[pl.BlockSpec((S, LANE), lambda b, pc: (b, OFF[P_AKV] // LANE + pc)),
                  pl.BlockSpec((1, 2, 1, kd), lambda b, pc: (pc // KV_A, 0, 0, 0)),
                  pl.BlockSpec((1, 2, kd, DH_A), lambda b, pc: (pc // KV_A, 0, 0, 0))],
        out_specs=pl.BlockSpec((1, 1, n_ch, DH_A), lambda b, pc: (b, pc, 0, 0)),
        out_shape=jax.ShapeDtypeStruct((B, 2 * KV_A, n_ch, DH_A), F32),
        compiler_params=pltpu.CompilerParams(dimension_semantics=("parallel", "arbitrary"),
                                             vmem_limit_bytes=VMEM_LIMIT),
        name="cmp_prompt",
    )(h2d, p4, w4)


def _split3(x):
    hi = x.astype(BF16)
    r1 = x - hi.astype(F32)
    mid = r1.astype(BF16)
    lo = (r1 - mid.astype(F32)).astype(BF16)
    return hi, mid, lo


def _block_select(psum, qpos, n_cmp, n_sel):
    rows, ncp = psum.shape
    jj = _iota((ncp, LANE), 0) * CMP_STRIDE
    ii = _iota((ncp, LANE), 1) * SEL_BLOCK
    ov = jnp.maximum(jnp.minimum(jj + CMP_LEN, ii + SEL_BLOCK) - jnp.maximum(jj, ii), 0)
    ovm = (ov.astype(F32) * (1.0 / CMP_LEN)).astype(BF16)
    imp = None
    for part in _split3(psum):
        t = jnp.dot(part, ovm, preferred_element_type=F32)
        imp = t if imp is None else imp + t
    ib = _iota((rows, LANE), 1)
    cur = qpos // SEL_BLOCK
    valid = (ib * SEL_BLOCK <= qpos) & (ib < n_sel)
    forced = (ib == 0) | (ib == cur) | (ib == cur - 1)
    imp = jnp.where(valid, imp + jnp.where(forced, FORCE_BONUS, 0.0), -jnp.inf)
    rank = jnp.zeros((rows, LANE), F32)
    for j in range(n_sel):
        col = imp[:, j:j + 1]
        ahead = (col > imp) | ((col == imp) & (ib > j))
        rank = rank + jnp.where(ahead, 1.0, 0.0)
    return jnp.where((rank < min(SEL_TOP, n_sel)) & (ib < n_sel), 1.0, 0.0)


def _cmpsel_prompt_kernel(slopes_ref, q_ref, kc_ref, ocmp_ref, sel_ref, *, tq, n_ch, n_sel):
    c = pl.program_id(1)
    i = pl.program_id(2)
    scale = DH_A ** -0.5
    n_cmp = n_ch - 1
    kck = kc_ref[0, c].astype(BF16)
    kcv = kc_ref[0, KV_A + c].astype(BF16)
    qpos = i * tq + _iota((tq, 1), 0)
    col = _iota((1, n_ch), 1)
    dist = qpos - (col * CMP_STRIDE + (CMP_LEN - 1))
    mask = (dist >= 0) & (col < n_cmp)
    distf = dist.astype(F32)
    psum = jnp.zeros((tq, n_ch), F32)
    for g in range(G_A):
        qg = q_ref[:, g * DH_A:(g + 1) * DH_A].astype(BF16)
        s = _dot_nt(qg, kck) * scale - slopes_ref[c * G_A + g] * distf
        p = _softmax_rows(s, mask)
        ocmp_ref[:, g * DH_A:(g + 1) * DH_A] = jnp.dot(p.astype(BF16), kcv, preferred_element_type=F32)
        psum = psum + p
    sel_ref[0, 0] = _block_select(psum, qpos, n_cmp, n_sel)


def _cmpsel_prompt(h2d, kc, B, S):
    tq = 256 if S % 256 == 0 else 128
    nq = S // tq
    n_ch = S // CMP_STRIDE
    n_sel = S // SEL_BLOCK
    gw = G_A * DH_A
    return pl.pallas_call(
        functools.partial(_cmpsel_prompt_kernel, tq=tq, n_ch=n_ch, n_sel=n_sel),
        grid=(B, KV_A, nq),
        in_specs=[pl.BlockSpec(memory_space=pltpu.SMEM),
                  pl.BlockSpec((tq, gw), lambda b, c, i: (b * nq + i, OFF[P_AQ] // gw + c)),
                  pl.BlockSpec((1, 2 * KV_A, n_ch, DH_A), lambda b, c, i: (b, 0, 0, 0))],
        out_specs=[pl.BlockSpec((tq, gw), lambda b, c, i: (b * nq + i, c)),
                   pl.BlockSpec((1, 1, tq, LANE), lambda b, c, i: (b, c, i, 0))],
        out_shape=[jax.ShapeDtypeStruct((B * S, W_A), F32), jax.ShapeDtypeStruct((B, KV_A, S, LANE), F32)],
        compiler_params=pltpu.CompilerParams(dimension_semantics=("parallel", "parallel", "arbitrary"),
                                             vmem_limit_bytes=VMEM_LIMIT),
        name="cmpsel_prompt",
    )(_alibi_slopes(H_A), h2d, kc)


def _gate(misc, lane_idx, col):
    return jnp.sum(jnp.where(lane_idx == col, _sigmoid(misc), 0.0), axis=-1, keepdims=True)


def _slcwin_prompt_kernel(slopes_ref, q_ref, misc_ref, ocmp_ref, sel_ref, sk_ref, sv_ref, wk_ref, wv_ref, o_ref, *, tq, nq):
    hh = pl.program_id(1)
    i = pl.program_id(2)
    scale = DH_A ** -0.5
    slope = slopes_ref[hh]

    def body(j):
        q0 = j * tq
        kmax = q0 + tq
        q = q_ref[...].astype(BF16)
        qpos = q0 + _iota((tq, 1), 0)
        ex = jnp.where(_iota((LANE, kmax), 1) // SEL_BLOCK == _iota((LANE, kmax), 0), 1.0, 0.0).astype(BF16)
        keym = jnp.dot(sel_ref[0, 0].astype(BF16), ex, preferred_element_type=F32)
        dist = qpos - _iota((1, kmax), 1)
        mask = (keym > 0.5) & (dist >= 0)
        s = _dot_nt(q, sk_ref[0:kmax, :].astype(BF16)) * scale - slope * dist.astype(F32)
        o_slc = jnp.dot(_softmax_rows(s, mask).astype(BF16), sv_ref[0:kmax, :].astype(BF16), preferred_element_type=F32)
        ws = max(0, q0 - WINDOW)
        dw = qpos - (ws + _iota((1, kmax - ws), 1))
        mw = (dw >= 0) & (dw <= WINDOW)
        sw = _dot_nt(q, wk_ref[ws:kmax, :].astype(BF16)) * scale - slope * dw.astype(F32)
        o_win = jnp.dot(_softmax_rows(sw, mw).astype(BF16), wv_ref[ws:kmax, :].astype(BF16), preferred_element_type=F32)
        misc = misc_ref[...]
        lane = _iota((tq, LANE), 1)
        o_ref[...] = (_gate(misc, lane, 3 * hh) * ocmp_ref[...] + _gate(misc, lane, 3 * hh + 1) * o_slc
                      + _gate(misc, lane, 3 * hh + 2) * o_win)

    for j in range(nq):
        pl.when(i == j)(functools.partial(body, j))


def _slcwin_prompt(h2d, ocmp, sel, B, S):
    tq = 256 if S % 256 == 0 else 128
    nq = S // tq
    akv = OFF[P_AKV] // LANE
    row = lambda w, cfn: pl.BlockSpec((tq, w), lambda b, h, i: (b * nq + i, cfn(h)))
    kvs = lambda piece: pl.BlockSpec((S, LANE), lambda b, h, i: (b, akv + piece * KV_A + h // G_A))
    return pl.pallas_call(
        functools.partial(_slcwin_prompt_kernel, tq=tq, nq=nq),
        grid=(B, H_A, nq),
        in_specs=[pl.BlockSpec(memory_space=pltpu.SMEM),
                  row(LANE, lambda h: OFF[P_AQ] // LANE + h),
                  row(LANE, lambda h: OFF[P_AG] // LANE),
                  row(LANE, lambda h: h),
                  pl.BlockSpec((1, 1, tq, LANE), lambda b, h, i: (b, h // G_A, i, 0)),
                  kvs(2), kvs(3), kvs(4), kvs(5)],
        out_specs=row(LANE, lambda h: h),
        out_shape=jax.ShapeDtypeStruct((B * S, W_A), F32),
        compiler_params=pltpu.CompilerParams(dimension_semantics=("parallel", "parallel", "arbitrary"),
                                             vmem_limit_bytes=VMEM_LIMIT),
        name="slcwin_prompt",
    )(_alibi_slopes(H_A), h2d, h2d, ocmp, sel, h2d, h2d, h2d, h2d)


def _nsa_prompt(h2d, B, S, w_cmp, cmp_pos):
    kc = _cmp_prompt(h2d, B, S, w_cmp, cmp_pos)
    ocmp, sel = _cmpsel_prompt(h2d, kc, B, S)
    return _slcwin_prompt(h2d, ocmp, sel, B, S)


def _dsa_sample_kernel(pt_ref, iq_ref, iw_ref, q_ref, new_ref, *rest, n_pages, ds, k_sel):
    page_refs = rest[:n_pages]
    o_ref = rest[n_pages]
    score_ref, sel_ref, s_ref = rest[n_pages + 1:]
    past = n_pages * PAGE_SIZE
    width = past + LANE
    lane = _iota((1, LANE), 1)
    new = new_ref[0]
    iqf = iq_ref[0]
    iq = iqf.astype(BF16)
    iw = iw_ref[0] * H_I ** -0.5
    score_ref[...] = jnp.full(score_ref.shape, -jnp.inf, F32)
    for p, pref in enumerate(page_refs):
        ik = pref[0, 0, 2].astype(BF16)
        rel = jnp.maximum(_dot_nt(iq, ik) * D_I ** -0.5, 0.0) * iw
        score_ref[0:ds, p * PAGE_SIZE:(p + 1) * PAGE_SIZE] = jnp.sum(rel.reshape(ds, H_I, PAGE_SIZE), axis=1)
    qi = _iota((ds, 1), 0)
    tile = jnp.full((ds, LANE), -jnp.inf, F32)
    for t in range(ds):
        rel = jnp.maximum(jnp.sum(iqf * new[t:t + 1, 2 * DH_B:3 * DH_B], axis=-1, keepdims=True) * D_I ** -0.5, 0.0) * iw
        sc = jnp.sum(rel.reshape(ds, H_I, 1), axis=1)
        tile = jnp.where((lane == t) & (qi >= t), sc, tile)
    score_ref[0:ds, past:width] = tile
    _topk_select(_sort_key(score_ref[...]), k_sel, sel_ref, width, two_bits=True)
    nr = ds * H_B
    row = _iota((nr, 1), 0)
    slope = _head_slopes(row % H_B, H_B)
    qpos = past + row // H_B
    qf = q_ref[0]
    q = qf.astype(BF16)
    scale = DH_B ** -0.5

    def sel_rows(c0):
        s = sel_ref[0:ds, c0:c0 + LANE]
        return jnp.broadcast_to(s[:, None, :], (ds, H_B, LANE)).reshape(nr, LANE) > 0.0

    for p, pref in enumerate(page_refs):
        k = pref[0, 0, 0].astype(BF16)
        dist = qpos - (p * PAGE_SIZE + lane)
        s = _dot_nt(q, k) * scale - slope * dist.astype(F32)
        s_ref[:, p * PAGE_SIZE:(p + 1) * PAGE_SIZE] = jnp.where(sel_rows(p * PAGE_SIZE) & (dist >= 0), s, -jnp.inf)
    tile = jnp.full((nr, LANE), -jnp.inf, F32)
    for t in range(ds):
        dist = qpos - (past + t)
        st = jnp.sum(qf * new[t:t + 1, 0:DH_B], axis=-1, keepdims=True) * scale - slope * dist.astype(F32)
        tile = jnp.where((lane == t) & (dist >= 0), st, tile)
    s_ref[:, past:width] = jnp.where(sel_rows(past), tile, -jnp.inf)
    s = s_ref[...]
    m = jnp.max(s, axis=-1, keepdims=True)
    m = jnp.where(m == -jnp.inf, 0.0, m)
    e = jnp.exp(s - m)
    pr = e / jnp.maximum(jnp.sum(e, axis=-1, keepdims=True), 1e-30)
    o = jnp.zeros((nr, DH_B), F32)
    for p, pref in enumerate(page_refs):
        v = pref[0, 0, 1].astype(BF16)
        o = o + jnp.dot(pr[:, p * PAGE_SIZE:(p + 1) * PAGE_SIZE].astype(BF16), v, preferred_element_type=F32)
    for t in range(ds):
        o = o + pr[:, past + t:past + t + 1] * new[t:t + 1, DH_B:2 * DH_B]
    o_ref[0] = o


def _dsa_sample(hs, cache_dsa, page_table, li):
    DB, ds, _ = hs.shape
    n_pages = page_table.shape[1]
    past = n_pages * PAGE_SIZE
    width = past + LANE
    k_sel = min(TOPK_MAX, (past + ds) // 4)
    iq3 = _piece(hs, P_BIQ).reshape(DB, ds * H_I, D_I)
    iw3 = _piece(hs, P_BIW).reshape(DB, ds * H_I, 1)
    q3 = _piece(hs, P_BQ).reshape(DB, ds * H_B, DH_B)
    new3 = hs[..., OFF[P_BKV]:OFF[P_BIK] + D_I]
    per_b = lambda shape: pl.BlockSpec((1,) + shape, lambda b, pt: (b, 0, 0))
    page_spec = lambda p: pl.BlockSpec((1, 1, 3, PAGE_SIZE, DH_B), lambda b, pt, p=p: (li, pt[b, p], 0, 0, 0))
    grid_spec = pltpu.PrefetchScalarGridSpec(
        num_scalar_prefetch=1,
        grid=(DB,),
        in_specs=[per_b((ds * H_I, D_I)), per_b((ds * H_I, 1)), per_b((ds * H_B, DH_B)), per_b((ds, 3 * DH_B))]
        + [page_spec(p) for p in range(n_pages)],
        out_specs=per_b((ds * H_B, DH_B)),
        scratch_shapes=[pltpu.VMEM((8, width), F32), pltpu.VMEM((8, width), F32), pltpu.VMEM((ds * H_B, width), F32)],
    )
    o = pl.pallas_call(
        functools.partial(_dsa_sample_kernel, n_pages=n_pages, ds=ds, k_sel=k_sel),
        grid_spec=grid_spec,
        out_shape=jax.ShapeDtypeStruct((DB, ds * H_B, DH_B), F32),
        compiler_params=pltpu.CompilerParams(dimension_semantics=("parallel",), vmem_limit_bytes=VMEM_LIMIT),
        name="dsa_sample",
    )(page_table, iq3, iw3, q3, new3, *([cache_dsa] * n_pages))
    return o.reshape(DB, ds, W_B)


def _nsa_sample_kernel(pt_ref, q_ref, gate_ref, new_ref, wnew_ref, st_ref, pos_ref, w_ref, *rest, n_pages, ds, wbuf):
    page_refs = rest[:n_pages]
    o_ref = rest[n_pages]
    s_ref = rest[n_pages + 1]
    x_refs = rest[n_pages + 2:]
    past = n_pages * PAGE_SIZE
    width = past + LANE
    n_ch = past // CMP_STRIDE
    n_sel = -(-(past + ds) // SEL_BLOCK)
    nr = G_A * ds * KV_A
    scale = DH_A ** -0.5
    row = _iota((nr, 1), 0)
    grp = row % KV_A
    qpos = past + (row % (ds * KV_A)) // KV_A
    slope = _head_slopes(grp * G_A + row // (ds * KV_A), H_A)
    lane = _iota((1, LANE), 1)
    qf = q_ref[0]
    q = qf.astype(BF16)
    new = new_ref[0]

    def page_piece(pref, pc):
        return pref[0, 0, pl.ds(pc, PAGE_SIZE, stride=4 * KV_A), :]

    def state_piece(pc):
        return st_ref[0, 0, pl.ds(pc, wbuf, stride=2 * KV_A), :]

    def by_group(fn):
        out = fn(0)
        for c in range(1, KV_A):
            out = jnp.where(grp == c, fn(c), out)
        return out

    kc = []
    for pc, x_ref in enumerate(x_refs):
        for p, pref in enumerate(page_refs):
            x_ref[p * PAGE_SIZE:(p + 1) * PAGE_SIZE, :] = page_piece(pref, pc)
        a = jnp.concatenate([x_ref[pl.ds(l, n_ch, stride=CMP_STRIDE), :] for l in range(CMP_STRIDE)], axis=1)
        kc.append(_compress_chunks(a, pos_ref, w_ref, pc // KV_A).astype(BF16))
    col = _iota((1, n_ch), 1)
    dist_c = qpos - (col * CMP_STRIDE + (CMP_LEN - 1))
    mask_c = (dist_c >= 0) & (col < n_ch - 1)
    s = by_group(lambda c: _dot_nt(q, kc[c])) * scale - slope * dist_c.astype(F32)
    p_c = _softmax_rows(s, mask_c)
    p_cb = p_c.astype(BF16)
    o_cmp = by_group(lambda c: jnp.dot(p_cb, kc[KV_A + c], preferred_element_type=F32))
    rq = ds * KV_A
    psum = p_c[0:rq]
    for g in range(1, G_A):
        psum = psum + p_c[g * rq:(g + 1) * rq]
    sel8 = _block_select(psum, qpos[0:rq], n_ch - 1, n_sel)
    sel = jnp.concatenate([sel8] * G_A, axis=0)

    bpp = PAGE_SIZE // SEL_BLOCK
    for p, pref in enumerate(page_refs):
        keym = jnp.zeros((nr, PAGE_SIZE), F32)
        for t in range(bpp):
            keym = jnp.where(lane // SEL_BLOCK == t, sel[:, p * bpp + t:p * bpp + t + 1], keym)
        dist = qpos - (p * PAGE_SIZE + lane)
        sp = by_group(lambda c: _dot_nt(q, page_piece(pref, 2 * KV_A + c).astype(BF16)))
        sp = sp * scale - slope * dist.astype(F32)
        s_ref[:, p * PAGE_SIZE:(p + 1) * PAGE_SIZE] = jnp.where((keym > 0.5) & (dist >= 0), sp, -jnp.inf)
    tile = jnp.full((nr, LANE), -jnp.inf, F32)
    for t in range(ds):
        dist = qpos - (past + t)
        kt = by_group(lambda c: jnp.broadcast_to(new[t:t + 1, (2 * KV_A + c) * DH_A:(2 * KV_A + c + 1) * DH_A], (nr, DH_A)))
        st = jnp.sum(qf * kt, axis=-1, keepdims=True) * scale - slope * dist.astype(F32)
        tile = jnp.where((lane == t) & (dist >= 0), st, tile)
    s_ref[:, past:width] = jnp.where(sel[:, n_pages * bpp:n_pages * bpp + 1] > 0.5, tile, -jnp.inf)
    s = s_ref[...]
    m = jnp.max(s, axis=-1, keepdims=True)
    m = jnp.where(m == -jnp.inf, 0.0, m)
    e = jnp.exp(s - m)
    pr = e / jnp.maximum(jnp.sum(e, axis=-1, keepdims=True), 1e-30)
    o_slc = jnp.zeros((nr, DH_A), F32)
    for p, pref in enumerate(page_refs):
        pb = pr[:, p * PAGE_SIZE:(p + 1) * PAGE_SIZE].astype(BF16)
        o_slc = o_slc + by_group(lambda c: jnp.dot(
            pb, page_piece(pref, 3 * KV_A + c).astype(BF16), preferred_element_type=F32))
    for t in range(ds):
        vt = by_group(lambda c: jnp.broadcast_to(new[t:t + 1, (3 * KV_A + c) * DH_A:(3 * KV_A + c + 1) * DH_A], (nr, DH_A)))
        o_slc = o_slc + pr[:, past + t:past + t + 1] * vt

    wnew = wnew_ref[0]
    dw = qpos - (past - wbuf + _iota((1, wbuf), 1))
    sw = by_group(lambda c: _dot_nt(q, state_piece(c).astype(BF16)))
    sw = jnp.where((dw >= 0) & (dw <= WINDOW), sw * scale - slope * dw.astype(F32), -jnp.inf)
    tile = jnp.full((nr, LANE), -jnp.inf, F32)
    for t in range(ds):
        dist = qpos - (past + t)
        kt = by_group(lambda c: jnp.broadcast_to(wnew[t:t + 1, c * DH_A:(c + 1) * DH_A], (nr, DH_A)))
        st = jnp.sum(qf * kt, axis=-1, keepdims=True) * scale - slope * dist.astype(F32)
        tile = jnp.where((lane == t) & (dist >= 0) & (dist <= WINDOW), st, tile)
    m = jnp.maximum(jnp.max(sw, axis=-1, keepdims=True), jnp.max(tile, axis=-1, keepdims=True))
    m = jnp.where(m == -jnp.inf, 0.0, m)
    ew = jnp.exp(sw - m)
    et = jnp.exp(tile - m)
    den = jnp.maximum(jnp.sum(ew, axis=-1, keepdims=True) + jnp.sum(et, axis=-1, keepdims=True), 1e-30)
    pw = (ew / den).astype(BF16)
    o_win = by_group(lambda c: jnp.dot(pw, state_piece(KV_A + c).astype(BF16), preferred_element_type=F32))
    pt = et / den
    for t in range(ds):
        vt = by_group(lambda c: jnp.broadcast_to(wnew[t:t + 1, (KV_A + c) * DH_A:(KV_A + c + 1) * DH_A], (nr, DH_A)))
        o_win = o_win + pt[:, t:t + 1] * vt

    g = _sigmoid(gate_ref[0])
    o_ref[0] = g[:, 0:1] * o_cmp + g[:, 1:2] * o_slc + g[:, 2:3] * o_win


def _nsa_sample(hs, win_n, cache_nsa, state_win, page_table, w_cmp, cmp_pos, li):
    DB, ds, _ = hs.shape
    n_pages = page_table.shape[1]
    past = n_pages * PAGE_SIZE
    wbuf = state_win.shape[2] // (2 * KV_A)
    assert ds * KV_A == 8 and CMP_LEN == 2 * CMP_STRIDE
    nr = G_A * ds * KV_A
    regroup = lambda a: jnp.transpose(a.reshape(DB, ds, KV_A, G_A, -1), (0, 3, 1, 2, 4)).reshape(DB, nr, -1)
    q3 = regroup(_piece(hs, P_AQ))
    g3 = regroup(_piece(hs, P_AG))
    new3 = hs[..., OFF[P_AKV]:OFF[P_AKV] + 4 * KV_A * DH_A]
    wnew3 = win_n.reshape(DB, ds, 2 * KV_A * DH_A)
    w4, p4 = _cmp_weights(w_cmp, cmp_pos)
    per_b = lambda shape: pl.BlockSpec((1,) + shape, lambda b, pt: (b, 0, 0))
    full = lambda a: pl.BlockSpec(a.shape, lambda b, pt: (0,) * a.ndim)
    page_spec = lambda p: pl.BlockSpec((1, 1, PAGE_SIZE * 4 * KV_A, DH_A), lambda b, pt, p=p: (li, pt[b, p], 0, 0))
    grid_spec = pltpu.PrefetchScalarGridSpec(
        num_scalar_prefetch=1,
        grid=(DB,),
        in_specs=[per_b((nr, DH_A)), per_b((nr, 3)), per_b((ds, 4 * KV_A * DH_A)), per_b((ds, 2 * KV_A * DH_A)),
                  pl.BlockSpec((1, 1, wbuf * 2 * KV_A, DH_A), lambda b, pt: (li, b, 0, 0)), full(p4), full(w4)]
        + [page_spec(p) for p in range(n_pages)],
        out_specs=per_b((nr, DH_A)),
        scratch_shapes=[pltpu.VMEM((nr, past + LANE), F32)] + [pltpu.VMEM((past, DH_A), F32)] * (2 * KV_A),
    )
    o = pl.pallas_call(
        functools.partial(_nsa_sample_kernel, n_pages=n_pages, ds=ds, wbuf=wbuf),
        grid_spec=grid_spec,
        out_shape=jax.ShapeDtypeStruct((DB, nr, DH_A), F32),
        compiler_params=pltpu.CompilerParams(dimension_semantics=("parallel",), vmem_limit_bytes=VMEM_LIMIT),
        name="nsa_sample",
    )(page_table, q3, g3, new3, wnew3, state_win, p4, w4, *([cache_nsa] * n_pages))
    return jnp.transpose(o.reshape(DB, G_A, ds, KV_A, DH_A), (0, 2, 3, 1, 4)).reshape(DB, ds, W_A)


def _new_rows(h, B, S):
    a_kv = _piece(h, P_AKV).reshape(B, S, 6, KV_A, DH_A)
    dsa_rows = h[..., OFF[P_BKV]:OFF[P_BIK] + D_I].reshape(B, S, 3, DH_B)
    diff_rows = h[..., OFF[P_CK]:OFF[P_CV] + W_C].reshape(B, S, 2, H_C, 2 * D_C)
    return a_kv[:, :, :4], a_kv[:, :, 4:], dsa_rows, diff_rows


def _mix(x, h, o_a, o_b, o_c, wbr, wout, ln_g, ln_b):
    B, S, _ = x.shape
    m = B * S
    y = _merge(o_a.reshape(m, W_A), o_b.reshape(m, W_B), o_c.reshape(m, W_C), h.reshape(m, N_PROJ),
               x.reshape(m, D_MODEL), wbr, wout, ln_g, ln_b)
    return y.reshape(B, S, D_MODEL)


def kernel(x_prompt, x_sample, cache_nsa, cache_dsa, cache_diff, state_nsa_win, page_table,
           w_in, w_cmp, cmp_pos, lam, sub_g, w_branch, w_out, ln_g, ln_b):
    B, S, _ = x_prompt.shape
    DB, DS, _ = x_sample.shape
    wbuf = state_nsa_win.shape[2]
    assert wbuf <= page_table.shape[1] * PAGE_SIZE and DS <= CMP_STRIDE
    rows = lambda a: a.reshape(a.shape[:2] + (-1, a.shape[-1]))
    nsa_pages, win_state = rows(cache_nsa), rows(state_nsa_win)
    dsa_pages = jnp.transpose(cache_dsa, (0, 1, 3, 2, 4))
    xp, xs = x_prompt, x_sample
    outs = [[] for _ in range(8)]
    for li in range(DEPTH):
        w_l = _permute_w_in(w_in[li])
        wbr = w_branch[li].astype(BF16)
        wout = w_out[li].astype(BF16)
        hp = _project(xp.reshape(B * S, D_MODEL), w_l)
        oa_p = _nsa_prompt(hp, B, S, w_cmp[li], cmp_pos[li])
        ob_p = _dsa_prompt(hp, B, S)
        oc_p = _diff_prompt(hp, B, S, lam[li], sub_g[li], li)
        hp = hp.reshape(B, S, N_PROJ)
        nsa_r, win_r, dsa_r, diff_r = _new_rows(hp, B, S)
        xp = _mix(xp, hp, oa_p, ob_p, oc_p, wbr, wout, ln_g[li], ln_b[li])
        hs = _project(xs.reshape(DB * DS, D_MODEL), w_l).reshape(DB, DS, N_PROJ)
        nsa_n, win_n, dsa_n, diff_n = _new_rows(hs, DB, DS)
        oa_s = _nsa_sample(hs, win_n, nsa_pages, win_state, page_table, w_cmp[li], cmp_pos[li], li)
        ob_s = _dsa_sample(hs, dsa_pages, page_table, li)
        oc_s = _diffs(hs, cache_diff, page_table, lam[li], sub_g[li], li)
        xs = _mix(xs, hs, oa_s, ob_s, oc_s, wbr, wout, ln_g[li], ln_b[li])
        for lst, v in zip(outs, (nsa_r, nsa_n, dsa_r, dsa_n, diff_r, diff_n, win_r[:, S - min(WINDOW, S):], win_n)):
            lst.append(v)
    outs = [jnp.stack(l) for l in outs]
    assert DS <= wbuf
    outs[7] = jnp.concatenate([state_nsa_win[:, :, DS:], outs[7]], axis=2)
    return (xp, xs) + tuple(outs)
```
